```python
import jax, jax.numpy as jnp
from jax import lax
import numpy as np

D_MODEL = 4096
BATCH = 4
SEQ = 2048
DEPTH = 2
DEC_BATCH = 128
DEC_SEQ = 8
PAST_LEN = 16384
PAGE_SIZE = 128

HGRN_WIDTH = D_MODEL // 2
HGRN_HEAD_DIM = 128
HGRN_HEADS = HGRN_WIDTH // HGRN_HEAD_DIM
HGRN_CHUNK = 64
POOL_WIDTH = D_MODEL // 4
POOL_WINDOWS = (2, 4, 8, 16)
POOL_GROUPS = len(POOL_WINDOWS)
POOL_GROUP_WIDTH = POOL_WIDTH // POOL_GROUPS
POOL_BUF = max(POOL_WINDOWS) - 1
MEM_WIDTH = D_MODEL // 4
MEM_HEADS = 4
MEM_HEAD_DIM = MEM_WIDTH // MEM_HEADS
MEM_TOKENS = 256
N_BRANCHES = 3
D_FF = 4 * D_MODEL
IN_COLS = 4 * HGRN_WIDTH + POOL_WIDTH + MEM_WIDTH + N_BRANCHES * D_MODEL
SPLIT_POINTS = (HGRN_WIDTH, 2 * HGRN_WIDTH, 3 * HGRN_WIDTH, 4 * HGRN_WIDTH,
                4 * HGRN_WIDTH + POOL_WIDTH, 4 * HGRN_WIDTH + POOL_WIDTH + MEM_WIDTH)
EPS = 1e-6
F_FLOOR = 1e-30

kernel_name = 'hybrid_hgrn2_pool_memattn_step'


def _rmsnorm(x, g):
    x32 = x.astype(jnp.float32)
    y = x32 * lax.rsqrt(jnp.mean(x32 * x32, axis=-1, keepdims=True) + EPS)
    return (y * g.astype(jnp.float32)).astype(x.dtype)


def _hgrn2_recurrence(q, k, v, logf, s0):
    b_, t_, h_, _ = q.shape
    dv = v.shape[-1]
    c = min(HGRN_CHUNK, t_)
    pad = (-t_) % c
    n = (t_ + pad) // c

    def to_chunks(a):
        a = jnp.pad(a, ((0, 0), (0, pad), (0, 0), (0, 0)))
        return a.reshape(b_, n, c, h_, a.shape[-1]).transpose(1, 0, 3, 2, 4)

    causal = jnp.tril(jnp.ones((c, c), dtype=bool))[None, None, :, :, None]

    def step(state, inp):
        qc, kc, vc, fc = inp
        cum = jnp.cumsum(fc, axis=2)
        o_inter = jnp.einsum('bhtd,bhde->bhte', qc * jnp.exp(cum), state)
        diff = cum[:, :, :, None, :] - cum[:, :, None, :, :]
        rel = jnp.where(causal, jnp.exp(jnp.where(causal, diff, 0.0)), 0.0)
        scores = jnp.einsum('bhtd,bhsd,bhtsd->bhts', qc, kc, rel)
        o = o_inter + jnp.einsum('bhts,bhse->bhte', scores, vc)
        last = cum[:, :, -1, :]
        new_state = (jnp.exp(last)[..., None] * state
                     + jnp.einsum('bhsd,bhse->bhde', kc * jnp.exp(last[:, :, None, :] - cum), vc))
        return new_state, o

    s_fin, o = lax.scan(step, s0, (to_chunks(q), to_chunks(k), to_chunks(v), to_chunks(logf)))
    o = o.transpose(1, 0, 3, 2, 4).reshape(b_, n * c, h_, dv)[:, :t_]
    return o, s_fin


def _multiscale_pool(p, buf, n_past, w_grp, scale):
    b_, t_, _ = p.shape
    full = jnp.concatenate([buf.astype(p.dtype), p], axis=1)
    f32 = full.astype(jnp.float32)
    cs = jnp.concatenate([jnp.zeros((b_, 1, POOL_WIDTH), jnp.float32), jnp.cumsum(f32, axis=1)], axis=1)
    hi = cs[:, POOL_BUF + 1:POOL_BUF + 1 + t_]
    p32 = f32[:, POOL_BUF:]
    pos = n_past + jnp.arange(t_, dtype=jnp.int32)
    groups = []
    for g, w in enumerate(POOL_WINDOWS):
        sl = slice(g * POOL_GROUP_WIDTH, (g + 1) * POOL_GROUP_WIDTH)
        lo = cs[:, POOL_BUF + 1 - w:POOL_BUF + 1 - w + t_, sl]
        count = jnp.minimum(pos + 1, w).astype(jnp.float32)[None, :, None]
        groups.append((hi[..., sl] - lo) / count - p32[..., sl])
    pooled = jnp.stack(groups, axis=2)
    mixed = jnp.einsum('btgc,gcd->btgd', pooled, w_grp.astype(jnp.float32)).reshape(b_, t_, POOL_WIDTH)
    out = (mixed * scale.astype(jnp.float32)).astype(p.dtype)
    return out, full[:, -POOL_BUF:]


def _mem_kv(mem, g, w_kv):
    b_, n_, _ = mem.shape
    m = _rmsnorm(mem, g) @ w_kv
    k = m[..., :MEM_WIDTH].reshape(b_, n_, MEM_HEADS, MEM_HEAD_DIM)
    v = m[..., MEM_WIDTH:].reshape(b_, n_, MEM_HEADS, MEM_HEAD_DIM)
    return k, v


def _mem_attend(q, k, v):
    s = jnp.einsum('bthd,bmhd->bhtm', q.astype(jnp.float32), k.astype(jnp.float32)) * (MEM_HEAD_DIM ** -0.5)
    a = jax.nn.softmax(s, axis=-1)
    o = jnp.einsum('bhtm,bmhd->bthd', a, v.astype(jnp.float32))
    return o.reshape(q.shape[0], q.shape[1], MEM_WIDTH).astype(q.dtype)


def setup_inputs(seed: int = 0) -> dict:
    key = jax.random.key(seed)
    ks = jax.random.split(key, 24)
    f32 = jnp.float32

    def nrm(k, shape, scale):
        return jax.random.normal(k, shape, f32) * scale

    def gain(k, shape):
        return 1.0 + 0.1 * jax.random.normal(k, shape, f32)

    return {
        'x_prompt': nrm(ks[0], (BATCH, SEQ, D_MODEL), 1.0),
        'x_sample': nrm(ks[1], (DEC_BATCH, DEC_SEQ, D_MODEL), 1.0),
        'state_hgrn': nrm(ks[2], (DEPTH, DEC_BATCH, HGRN_HEADS, HGRN_HEAD_DIM, HGRN_HEAD_DIM), 0.5),
        'state_pool': nrm(ks[3], (DEPTH, DEC_BATCH, POOL_BUF, POOL_WIDTH), 1.0),
        'cache_mem_k': nrm(ks[4], (DEPTH, DEC_BATCH, MEM_TOKENS, MEM_HEADS, MEM_HEAD_DIM), 1.0),
        'cache_mem_v': nrm(ks[5], (DEPTH, DEC_BATCH, MEM_TOKENS, MEM_HEADS, MEM_HEAD_DIM), 1.0),
        'mem_prompt': nrm(ks[6], (BATCH, MEM_TOKENS, D_MODEL), 1.0),
        'norm_pre_mix': gain(ks[7], (DEPTH, D_MODEL)),
        'norm_post_mix': gain(ks[8], (DEPTH, D_MODEL)),
        'norm_pre_mlp': gain(ks[9], (DEPTH, D_MODEL)),
        'norm_post_mlp': gain(ks[10], (DEPTH, D_MODEL)),
        'norm_mem': gain(ks[11], (DEPTH, D_MODEL)),
        'w_in': nrm(ks[12], (DEPTH, D_MODEL, IN_COLS), D_MODEL ** -0.5),
        'hgrn_lb': nrm(ks[13], (DEPTH, HGRN_WIDTH), 0.5),
        'hgrn_out_norm': gain(ks[14], (DEPTH, HGRN_WIDTH)),
        'w_pool': nrm(ks[15], (DEPTH, POOL_GROUPS, POOL_GROUP_WIDTH, POOL_GROUP_WIDTH), POOL_GROUP_WIDTH ** -0.5),
        'pool_scale': gain(ks[16], (DEPTH, POOL_WIDTH)),
        'w_mem_kv': nrm(ks[17], (DEPTH, D_MODEL, 2 * MEM_WIDTH), D_MODEL ** -0.5),
        'w_branch_hgrn': nrm(ks[18], (DEPTH, HGRN_WIDTH, D_MODEL), HGRN_WIDTH ** -0.5),
        'w_branch_pool': nrm(ks[19], (DEPTH, POOL_WIDTH, D_MODEL), POOL_WIDTH ** -0.5),
        'w_branch_mem': nrm(ks[20], (DEPTH, MEM_WIDTH, D_MODEL), MEM_WIDTH ** -0.5),
        'w_out': nrm(ks[21], (DEPTH, D_MODEL, D_MODEL), D_MODEL ** -0.5),
        'w_up': nrm(ks[22], (DEPTH, D_MODEL, D_FF), D_MODEL ** -0.5),
        'w_down': nrm(ks[23], (DEPTH, D_FF, D_MODEL), D_FF ** -0.5),
    }


def reference(x_prompt, x_sample, state_hgrn, state_pool, cache_mem_k, cache_mem_v, mem_prompt,
              norm_pre_mix, norm_post_mix, norm_pre_mlp, norm_post_mlp, norm_mem,
              w_in, hgrn_lb, hgrn_out_norm, w_pool, pool_scale, w_mem_kv,
              w_branch_hgrn, w_branch_pool, w_branch_mem, w_out, w_up, w_down):
    f32 = jnp.float32
    lb_soft = jax.nn.softmax(hgrn_lb.astype(f32), axis=0)
    lb_all = jnp.cumsum(lb_soft, axis=0) - lb_soft[0:1]

    def run_layer(l, x, s0, buf, mem_k, mem_v, n_past):
        b_, t_, _ = x.shape
        h = _rmsnorm(x, norm_pre_mix[l])
        u = h @ w_in[l]
        uq, uf, ui, ug, up, uc, ugate = jnp.split(u, SPLIT_POINTS, axis=-1)
        hshape = (b_, t_, HGRN_HEADS, HGRN_HEAD_DIM)
        lb = lb_all[l].reshape(HGRN_HEADS, HGRN_HEAD_DIM)
        q = jax.nn.silu(uq.astype(f32)).reshape(hshape) * (HGRN_HEAD_DIM ** -0.5)
        f = lb + (1.0 - lb) * jax.nn.sigmoid(uf.astype(f32).reshape(hshape))
        logf = jnp.log(jnp.maximum(f, F_FLOOR))
        k = 1.0 - f
        o, s_new = _hgrn2_recurrence(q, k, ui.astype(f32).reshape(hshape), logf, s0.astype(f32))
        o = o * lax.rsqrt(jnp.mean(o * o, axis=-1, keepdims=True) + EPS) \
            * hgrn_out_norm[l].astype(f32).reshape(HGRN_HEADS, HGRN_HEAD_DIM)
        o = (o.reshape(b_, t_, HGRN_WIDTH) * jax.nn.silu(ug.astype(f32))).astype(x.dtype)
        pooled, buf_new = _multiscale_pool(up, buf, n_past, w_pool[l], pool_scale[l])
        mem_o = _mem_attend(uc.reshape(b_, t_, MEM_HEADS, MEM_HEAD_DIM), mem_k, mem_v)
        gates = jax.nn.sigmoid(ugate.reshape(b_, t_, N_BRANCHES, D_MODEL))
        merged = (gates[:, :, 0] * (o @ w_branch_hgrn[l])
                  + gates[:, :, 1] * (pooled @ w_branch_pool[l])
                  + gates[:, :, 2] * (mem_o @ w_branch_mem[l]))
        x = x + _rmsnorm(merged @ w_out[l], norm_post_mix[l])
        hf = _rmsnorm(x, norm_pre_mlp[l]) @ w_up[l]
        ff = jnp.square(jax.nn.relu(hf)) @ w_down[l]
        x = x + _rmsnorm(ff, norm_post_mlp[l])
        return x, s_new.astype(x.dtype), buf_new

    b_p = x_prompt.shape[0]
    y_p = x_prompt
    y_s = x_sample
    hgrn_p, pool_p, mk_p, mv_p, hgrn_s, pool_s = [], [], [], [], [], []
    for l in range(DEPTH):
        mk, mv = _mem_kv(mem_prompt, norm_mem[l], w_mem_kv[l])
        s0 = jnp.zeros((b_p, HGRN_HEADS, HGRN_HEAD_DIM, HGRN_HEAD_DIM), f32)
        buf0 = jnp.zeros((b_p, POOL_BUF, POOL_WIDTH), x_prompt.dtype)
        y_p, sp, bp = run_layer(l, y_p, s0, buf0, mk, mv, 0)
        y_s, ss, bs = run_layer(l, y_s, state_hgrn[l], state_pool[l], cache_mem_k[l], cache_mem_v[l], PAST_LEN)
        hgrn_p.append(sp)
        pool_p.append(bp)
        mk_p.append(mk)
        mv_p.append(mv)
        hgrn_s.append(ss)
        pool_s.append(bs)

    return (y_p, y_s, jnp.stack(hgrn_p), jnp.stack(pool_p), jnp.stack(mk_p), jnp.stack(mv_p),
            jnp.stack(hgrn_s), jnp.stack(pool_s))
```

```python
import functools

import jax
import jax.numpy as jnp
from jax import lax
from jax.experimental import pallas as pl
from jax.experimental.pallas import tpu as pltpu

F32 = jnp.float32
MXU_DTYPE = jnp.bfloat16
EPS = 1e-6
F_FLOOR = 1e-30
NEG_BIG = -1e30
HGRN_HEAD_DIM = 128
POOL_WINDOWS = (2, 4, 8, 16)
POOL_HIST = 16
MEM_HEADS = 4
N_BRANCHES = 3
V7X_VMEM_LIMIT_BYTES = 56 * 1024 * 1024
HGRN_CHUNK = 128
HGRN_DIAG = 8


def _params(*sem):
    return pltpu.CompilerParams(dimension_semantics=sem, vmem_limit_bytes=V7X_VMEM_LIMIT_BYTES)


def _tile(n, t):
    t = min(n, t)
    assert n % t == 0, (n, t)
    return t


def _rms_scale(x):
    return lax.rsqrt(jnp.mean(x * x, axis=-1, keepdims=True) + EPS)


def _rmsnorm_kernel(x_ref, g_ref, o_ref):
    x = x_ref[...]
    o_ref[...] = (x * _rms_scale(x) * g_ref[...]).astype(o_ref.dtype)


def _rmsnorm(x, g, tm=256):
    m, d = x.shape
    tm = _tile(m, tm)
    return pl.pallas_call(
        _rmsnorm_kernel,
        out_shape=jax.ShapeDtypeStruct((m, d), MXU_DTYPE),
        grid=(m // tm,),
        in_specs=[pl.BlockSpec((tm, d), lambda i: (i, 0)), pl.BlockSpec((1, d), lambda i: (0, 0))],
        out_specs=pl.BlockSpec((tm, d), lambda i: (i, 0)),
        compiler_params=_params("parallel"),
        name="rmsnorm",
    )(x, g.reshape(1, d))


def _proj_kernel(a_ref, w_ref, o_ref, *, relu2):
    acc = jnp.dot(a_ref[...], w_ref[...].astype(MXU_DTYPE), preferred_element_type=F32)
    if relu2:
        acc = jnp.square(jnp.maximum(acc, 0.0))
    o_ref[...] = acc.astype(o_ref.dtype)


def _proj(a, w, *, out_dtype, relu2=False, tm=2304, tn=256):
    m, k = a.shape
    n = w.shape[1]
    tm, tn = _tile(m, tm), _tile(n, tn)
    return pl.pallas_call(
        functools.partial(_proj_kernel, relu2=relu2),
        out_shape=jax.ShapeDtypeStruct((m, n), out_dtype),
        grid=(m // tm, n // tn),
        in_specs=[pl.BlockSpec((tm, k), lambda i, j: (i, 0), pipeline_mode=pl.Buffered(1)),
                  pl.BlockSpec((k, tn), lambda i, j: (0, j))],
        out_specs=pl.BlockSpec((tm, tn), lambda i, j: (i, j)),
        compiler_params=_params("parallel", "arbitrary"),
        name="proj",
    )(a, w)


def _merge_kernel(ah_ref, ap_ref, am_ref, wh_ref, wp_ref, wm_ref, gh_ref, gp_ref, gm_ref, o_ref):
    def branch(a_ref, w_ref, g_ref):
        y = jnp.dot(a_ref[...], w_ref[...].astype(MXU_DTYPE), preferred_element_type=F32)
        return jax.nn.sigmoid(g_ref[...]) * y
    o_ref[...] = (branch(ah_ref, wh_ref, gh_ref) + branch(ap_ref, wp_ref, gp_ref)
                  + branch(am_ref, wm_ref, gm_ref)).astype(o_ref.dtype)


def _merge(o_hgrn, pooled, mem_o, w_h, w_p, w_m, u, gate_col, *, tm=1024, tn=256):
    m = o_hgrn.shape[0]
    d = w_h.shape[1]
    tm, tn = _tile(m, tm), _tile(d, tn)
    assert gate_col % tn == 0
    gc, nd = gate_col // tn, d // tn
    a_spec = lambda a: pl.BlockSpec((tm, a.shape[1]), lambda i, j: (i, 0))
    w_spec = lambda w: pl.BlockSpec((w.shape[0], tn), lambda i, j: (0, j))
    g_spec = lambda b: pl.BlockSpec((tm, tn), lambda i, j: (i, gc + b * nd + j))
    return pl.pallas_call(
        _merge_kernel,
        out_shape=jax.ShapeDtypeStruct((m, d), MXU_DTYPE),
        grid=(m // tm, nd),
        in_specs=[a_spec(o_hgrn), a_spec(pooled), a_spec(mem_o), w_spec(w_h), w_spec(w_p), w_spec(w_m),
                  g_spec(0), g_spec(1), g_spec(2)],
        out_specs=pl.BlockSpec((tm, tn), lambda i, j: (i, j)),
        compiler_params=_params("parallel", "arbitrary"),
        name="merge",
    )(o_hgrn, pooled, mem_o, w_h, w_p, w_m, u, u, u)


def _resid_kernel(a_ref, w_ref, x_ref, gpost_ref, gnext_ref, ox_ref, *rest, nk, nc, tc, want_h):
    if want_h:
        oh_ref, acc_ref, s1_ref, s2_ref = rest
    else:
        acc_ref, s1_ref, s2_ref = rest
    s = pl.program_id(1)
    n_total = nc * tc

    def partial_products():
        a = a_ref[...]
        w = w_ref[...].astype(MXU_DTYPE)
        return [jnp.dot(a, w[:, c * tc:(c + 1) * tc], preferred_element_type=F32) for c in range(nc)]

    @pl.when(s == 0)
    def _():
        for c, y in enumerate(partial_products()):
            acc_ref[c] = y

    @pl.when(jnp.logical_and(s > 0, s < nk))
    def _():
        for c, y in enumerate(partial_products()):
            acc_ref[c] += y

    @pl.when(s == nk)
    def _():
        ss = jnp.zeros(s1_ref.shape, F32)
        for c in range(nc):
            y = acc_ref[c]
            ss = ss + jnp.sum(y * y, axis=-1, keepdims=True)
        s1_ref[...] = lax.rsqrt(ss / n_total + EPS)
        s2_ref[...] = jnp.zeros(s2_ref.shape, F32)

    @pl.when(jnp.logical_and(s >= nk, s < nk + nc))
    def _():
        c = s - nk
        xn = x_ref[...] + acc_ref[c] * s1_ref[...] * gpost_ref[...]
        ox_ref[...] = xn
        if want_h:
            acc_ref[c] = xn
            s2_ref[...] += jnp.sum(xn * xn, axis=-1, keepdims=True)

    if want_h:
        @pl.when(s >= nk + nc)
        def _():
            c = s - nk - nc
            scale = lax.rsqrt(s2_ref[...] / n_total + EPS)
            oh_ref[...] = (acc_ref[c] * scale * gnext_ref[...]).astype(oh_ref.dtype)


def _resid(a, w, x, g_post, g_next, *, tm=1024, tk=512, tc=512):
    m, k = a.shape
    n = w.shape[1]
    tm, tk, tc = _tile(m, tm), _tile(k, tk), _tile(n, tc)
    nk, nc = k // tk, n // tc
    want_h = g_next is not None
    if not want_h:
        g_next = g_post
    steps = nk + nc * (2 if want_h else 1)
    kk = lambda s: jnp.minimum(s, nk - 1)
    c1 = lambda s: jnp.clip(s - nk, 0, nc - 1)
    c2 = lambda s: jnp.clip(s - nk - nc, 0, nc - 1)
    out_shape = [jax.ShapeDtypeStruct((m, n), F32)]
    out_specs = [pl.BlockSpec((tm, tc), lambda i, s: (i, c1(s)))]
    if want_h:
        out_shape.append(jax.ShapeDtypeStruct((m, n), MXU_DTYPE))
        out_specs.append(pl.BlockSpec((tm, tc), lambda i, s: (i, c2(s))))
    outs = pl.pallas_call(
        functools.partial(_resid_kernel, nk=nk, nc=nc, tc=tc, want_h=want_h),
        out_shape=out_shape,
        grid=(m // tm, steps),
        in_specs=[pl.BlockSpec((tm, tk), lambda i, s: (i, kk(s))),
                  pl.BlockSpec((tk, n), lambda i, s: (kk(s), 0)),
                  pl.BlockSpec((tm, tc), lambda i, s: (i, c1(s))),
                  pl.BlockSpec((1, tc), lambda i, s: (0, c1(s))),
                  pl.BlockSpec((1, tc), lambda i, s: (0, c2(s)))],
        out_specs=out_specs,
        scratch_shapes=[pltpu.VMEM((nc, tm, tc), F32), pltpu.VMEM((tm, 1), F32), pltpu.VMEM((tm, 1), F32)],
        compiler_params=_params("parallel", "arbitrary"),
        name="resid",
    )(a, w, x, g_post.reshape(1, n), g_next.reshape(1, n))
    return (outs[0], outs[1]) if want_h else (outs[0], None)


def _hgrn_gates(uq, uf, lb):
    f = lb + (1.0 - lb) * jax.nn.sigmoid(uf)
    logf = jnp.log(jnp.maximum(f, F_FLOOR))
    q = jax.nn.silu(uq) * (HGRN_HEAD_DIM ** -0.5)
    return q, 1.0 - f, logf


def _hgrn_finish(o, ug, gain, dtype):
    o = o * _rms_scale(o) * gain
    return (o * jax.nn.silu(ug)).astype(dtype)


def _split3(x):
    h1 = x.astype(MXU_DTYPE)
    r = x - h1.astype(F32)
    h2 = r.astype(MXU_DTYPE)
    h3 = (r - h2.astype(F32)).astype(MXU_DTYPE)
    return h1, h2, h3


def _diag_offsets(q, k, cum, row, block):
    yield 0, jnp.sum(q * k, axis=-1, keepdims=True)
    for j in range(1, block):
        inside = (row % block) >= j
        d = jnp.where(inside, cum - pltpu.roll(cum, j, 0), NEG_BIG)
        yield j, jnp.sum(q * pltpu.roll(k, j, 0) * jnp.exp(d), axis=-1, keepdims=True)


def _hgrn_chunk(q, k, v, logf, st):
    c = q.shape[0]
    row = lax.broadcasted_iota(jnp.int32, (c, 1), 0)
    rr = lax.broadcasted_iota(jnp.int32, (c, c), 0)
    cc = lax.broadcasted_iota(jnp.int32, (c, c), 1)
    tri = (cc <= rr).astype(MXU_DTYPE)
    cum = sum(jnp.dot(tri, h, preferred_element_type=F32) for h in _split3(logf)[::-1])

    scores = jnp.zeros((c, c), F32)
    for j, rs in _diag_offsets(q, k, cum, row, HGRN_DIAG):
        scores = scores + jnp.where(cc == rr - j, rs, 0.0)
    b = HGRN_DIAG
    while b < c:
        c3 = cum.reshape(c // (2 * b), 2 * b, cum.shape[1])
        e = jnp.exp(-jnp.abs(c3 - c3[:, b - 1:b, :])).reshape(cum.shape)
        odd = ((row // b) % 2) == 1
        qt = jnp.where(odd, q * e, 0.0).astype(MXU_DTYPE)
        kt = jnp.where(odd, 0.0, k * e).astype(MXU_DTYPE)
        s = lax.dot_general(qt, kt, (((1,), (1,)), ((), ())), preferred_element_type=F32)
        scores = scores + jnp.where((rr // (2 * b)) == (cc // (2 * b)), s, 0.0)
        b *= 2

    vm = v.astype(MXU_DTYPE)
    qd = (q * jnp.exp(cum)).astype(MXU_DTYPE)
    o = lax.dot_general(qd, st.astype(MXU_DTYPE), (((1,), (1,)), ((), ())), preferred_element_type=F32)
    o = o + jnp.dot(scores.astype(MXU_DTYPE), vm, preferred_element_type=F32)
    last = cum[c - 1:c, :]
    kd = (k * jnp.exp(last - cum)).astype(MXU_DTYPE)
    st_new = st * jnp.exp(last) + jnp.dot(v.T.astype(MXU_DTYPE), kd, preferred_element_type=F32)
    return o, st_new


def _hgrn_prompt_kernel(uq_ref, uf_ref, ui_ref, ug_ref, lb_ref, gain_ref, o_ref, s_ref, st_ref, *, chunk):
    n = pl.program_id(2)

    @pl.when(n == 0)
    def _():
        st_ref[...] = jnp.zeros(st_ref.shape, F32)

    lb, gain = lb_ref[...], gain_ref[...]
    for ci in range(uq_ref.shape[0] // chunk):
        rows = pl.ds(ci * chunk, chunk)
        q, k, logf = _hgrn_gates(uq_ref[rows, :], uf_ref[rows, :], lb)
        o, st_new = _hgrn_chunk(q, k, ui_ref[rows, :], logf, st_ref[...])
        st_ref[...] = st_new
        o_ref[rows, :] = _hgrn_finish(o, ug_ref[rows, :], gain, o_ref.dtype)

    @pl.when(n == pl.num_programs(2) - 1)
    def _():
        s_ref[...] = st_ref[...].T


def _hgrn_prompt(u, lb, gain, batch, seq, width, *, tb=512):
    dk = HGRN_HEAD_DIM
    heads = width // dk
    chunk = min(HGRN_CHUNK, seq)
    tb = _tile(seq, tb)
    assert tb % chunk == 0
    nt = seq // tb
    u_spec = lambda c: pl.BlockSpec((tb, dk), lambda b, h, n: (b * nt + n, c * heads + h))
    vec_spec = pl.BlockSpec((1, dk), lambda b, h, n: (0, h))
    return pl.pallas_call(
        functools.partial(_hgrn_prompt_kernel, chunk=chunk),
        out_shape=[jax.ShapeDtypeStruct((batch * seq, width), MXU_DTYPE),
                   jax.ShapeDtypeStruct((batch, heads, dk, dk), F32)],
        grid=(batch, heads, nt),
        in_specs=[u_spec(0), u_spec(1), u_spec(2), u_spec(3), vec_spec, vec_spec],
        out_specs=[pl.BlockSpec((tb, dk), lambda b, h, n: (b * nt + n, h)),
                   pl.BlockSpec((None, None, dk, dk), lambda b, h, n: (b, h, 0, 0))],
        scratch_shapes=[pltpu.VMEM((dk, dk), F32)],
        compiler_params=_params("parallel", "parallel", "arbitrary"),
        name="hgrn_prompt",
    )(u, u, u, u, lb.reshape(1, width), gain.reshape(1, width))


def _hgrn_sample_kernel(uq_ref, uf_ref, ui_ref, ug_ref, lb_ref, gain_ref, s0_ref, o_ref, s_ref, *, t):
    nb = s0_ref.shape[0]
    width = uq_ref.shape[1]
    dk = HGRN_HEAD_DIM
    heads = width // dk
    pad = dk - 2 * t
    assert pad >= 0
    lb = lb_ref[...]
    row = lax.broadcasted_iota(jnp.int32, (t, 1), 0)
    outs = [[] for _ in range(heads)]
    for sq in range(nb):
        rows = pl.ds(sq * t, t)
        q, k, logf = _hgrn_gates(uq_ref[rows, :], uf_ref[rows, :], lb)
        v = ui_ref[rows, :]
        cum = logf
        sh = 1
        while sh < t:
            cum = cum + jnp.where(row >= sh, pltpu.roll(cum, sh, 0), 0.0)
            sh *= 2
        last = cum[t - 1:t, :]
        qd = q * jnp.exp(cum)
        kd = k * jnp.exp(last - cum)
        dec = jnp.exp(last)
        for h in range(heads):
            cols = slice(h * dk, (h + 1) * dk)
            vh = v[:, cols]
            s0 = s0_ref[sq, h]
            o = jnp.dot(qd[:, cols].astype(MXU_DTYPE), s0.astype(MXU_DTYPE), preferred_element_type=F32)
            for j, rs in _diag_offsets(q[:, cols], k[:, cols], cum[:, cols], row, t):
                o = o + rs * (vh if j == 0 else pltpu.roll(vh, j, 0))
            outs[h].append(o)
            stack_t = jnp.concatenate([kd[:, cols], jnp.broadcast_to(dec[:, cols], (t, dk)),
                                       jnp.zeros((pad, dk), F32)], axis=0).T
            v_pad = jnp.concatenate([vh, jnp.zeros((dk - t, dk), F32)], axis=0)
            s_ref[sq, h] = stack_t[:, t:t + 1] * s0 + jnp.dot(stack_t.astype(MXU_DTYPE), v_pad.astype(MXU_DTYPE),
                                                               preferred_element_type=F32)
    for h in range(heads):
        cols = slice(h * dk, (h + 1) * dk)
        o = jnp.concatenate(outs[h], axis=0)
        o_ref[:, cols] = _hgrn_finish(o, ug_ref[:, cols], gain_ref[:, cols], o_ref.dtype)


def _hgrn_sample(u, lb, gain, s0, row0, batch, t, width, *, nb=2):
    dk = HGRN_HEAD_DIM
    heads = width // dk
    nb = _tile(batch, nb)
    assert row0 % (nb * t) == 0
    r0 = row0 // (nb * t)
    u_spec = lambda c: pl.BlockSpec((nb * t, width), lambda b: (r0 + b, c))
    vec_spec = pl.BlockSpec((1, width), lambda b: (0, 0))
    st_spec = pl.BlockSpec((nb, heads, dk, dk), lambda b: (b, 0, 0, 0))
    return pl.pallas_call(
        functools.partial(_hgrn_sample_kernel, t=t),
        out_shape=[jax.ShapeDtypeStruct((batch * t, width), MXU_DTYPE),
                   jax.ShapeDtypeStruct((batch, heads, dk, dk), F32)],
        grid=(batch // nb,),
        in_specs=[u_spec(0), u_spec(1), u_spec(2), u_spec(3), vec_spec, vec_spec, st_spec],
        out_specs=[pl.BlockSpec((nb * t, width), lambda b: (b, 0)), st_spec],
        compiler_params=_params("parallel"),
        name="hgrn_sample",
    )(u, u, u, u, lb.reshape(1, width), gain.reshape(1, width), s0)


def _pool_kernel(p_ref, buf_ref, w_ref, scale_ref, o_ref, nbuf_ref, ext_ref, *, n_past):
    n = pl.program_id(1)
    nb, tb, width = p_ref.shape
    gw = width // len(POOL_WINDOWS)
    hist = POOL_HIST

    @pl.when(n == 0)
    def _():
        ext_ref[:, 0:hist, :] = buf_ref[...]

    @pl.when(n > 0)
    def _():
        ext_ref[:, 0:hist, :] = ext_ref[:, tb:tb + hist, :]

    ext_ref[:, hist:hist + tb, :] = p_ref[...]
    pos = n_past + n * tb + lax.broadcasted_iota(jnp.int32, (1, tb, 1), 1)
    for g, win in enumerate(POOL_WINDOWS):
        cols = slice(g * gw, (g + 1) * gw)
        tot = ext_ref[:, hist:hist + tb, cols]
        x = tot
        for i in range(1, win):
            tot = tot + ext_ref[:, hist - i:hist - i + tb, cols]
        count = jnp.minimum(pos + 1, win).astype(F32)
        pooled = (tot / count - x).reshape(nb * tb, gw)
        mixed = jnp.dot(pooled.astype(MXU_DTYPE), w_ref[g].astype(MXU_DTYPE), preferred_element_type=F32)
        o_ref[:, cols] = (mixed * scale_ref[:, cols]).astype(o_ref.dtype)

    @pl.when(n == pl.num_programs(1) - 1)
    def _():
        nbuf_ref[...] = ext_ref[:, tb:tb + hist, :]


def _pool(u, buf, w_grp, scale, col, row0, batch, seq, n_past, *, nb, tb):
    groups, gw, _ = w_grp.shape
    width = groups * gw
    hist = POOL_HIST
    nb, tb = _tile(batch, nb), _tile(seq, tb)
    nt = seq // tb
    assert col % width == 0 and row0 % (nb * tb) == 0 and (nb == 1 or nt == 1) and tb % 8 == 0
    if buf is None:
        buf = jnp.zeros((batch, hist, width), F32)
    else:
        buf = jnp.pad(buf, ((0, 0), (1, 0), (0, 0)))
    u3 = u.reshape(u.shape[0] // tb, tb, u.shape[1])
    r0 = row0 // (nb * tb)
    mixed, nbuf = pl.pallas_call(
        functools.partial(_pool_kernel, n_past=n_past),
        out_shape=[jax.ShapeDtypeStruct((batch * seq, width), MXU_DTYPE),
                   jax.ShapeDtypeStruct((batch, hist, width), F32)],
        grid=(batch // nb, nt),
        in_specs=[pl.BlockSpec((nb, tb, width), lambda b, n: (r0 + b * nt + n, 0, col // width)),
                  pl.BlockSpec((nb, hist, width), lambda b, n: (b, 0, 0)),
                  pl.BlockSpec((groups, gw, gw), lambda b, n: (0, 0, 0)),
                  pl.BlockSpec((1, width), lambda b, n: (0, 0))],
        out_specs=[pl.BlockSpec((nb * tb, width), lambda b, n: (b * nt + n, 0)),
                   pl.BlockSpec((nb, hist, width), lambda b, n: (b, 0, 0))],
        scratch_shapes=[pltpu.VMEM((nb, hist + tb, width), F32)],
        compiler_params=_params("parallel", "arbitrary"),
        name="pool",
    )(u3, buf, w_grp, scale.reshape(1, width))
    return mixed, nbuf[:, 1:, :]


def _attend_kernel(q_ref, k_ref, v_ref, o_ref, *, tq):
    nb, _, width = k_ref.shape
    hd = width // MEM_HEADS
    for b in range(nb):
        rows = slice(b * tq, (b + 1) * tq)
        for h in range(MEM_HEADS):
            cols = slice(h * hd, (h + 1) * hd)
            q = q_ref[rows, cols].astype(MXU_DTYPE)
            s = lax.dot_general(q, k_ref[b, :, cols].astype(MXU_DTYPE), (((1,), (1,)), ((), ())),
                                preferred_element_type=F32) * (hd ** -0.5)
            e = jnp.exp(s - jnp.max(s, axis=-1, keepdims=True))
            a = e / jnp.sum(e, axis=-1, keepdims=True)
            o_ref[rows, cols] = jnp.dot(a.astype(MXU_DTYPE), v_ref[b, :, cols].astype(MXU_DTYPE),
                                        preferred_element_type=F32).astype(o_ref.dtype)


def _attend(u, k_arr, v_arr, k_blk, v_blk, width, col, row0, batch, seq, *, nb, tq):
    tokens = k_arr.shape[1]
    nb, tq = _tile(batch, nb), _tile(seq, tq)
    nt = seq // tq
    assert col % width == 0 and row0 % (nb * tq) == 0 and (nb == 1 or nt == 1)
    r0 = row0 // (nb * tq)
    return pl.pallas_call(
        functools.partial(_attend_kernel, tq=tq),
        out_shape=jax.ShapeDtypeStruct((batch * seq, width), MXU_DTYPE),
        grid=(batch // nb, nt),
        in_specs=[pl.BlockSpec((nb * tq, width), lambda b, n: (r0 + b * nt + n, col // width)),
                  pl.BlockSpec((nb, tokens, width), lambda b, n: (b, 0, k_blk)),
                  pl.BlockSpec((nb, tokens, width), lambda b, n: (b, 0, v_blk))],
        out_specs=pl.BlockSpec((nb * tq, width), lambda b, n: (b * nt + n, 0)),
        compiler_params=_params("parallel", "arbitrary"),
        name="attend",
    )(u, k_arr, v_arr)


PAST_LEN = 16384


def kernel(x_prompt, x_sample, state_hgrn, state_pool, cache_mem_k, cache_mem_v, mem_prompt, norm_pre_mix, norm_post_mix, norm_pre_mlp, norm_post_mlp, norm_mem, w_in, hgrn_lb, hgrn_out_norm, w_pool, pool_scale, w_mem_kv, w_branch_hgrn, w_branch_pool, w_branch_mem, w_out, w_up, w_down):
    bp, sp, d = x_prompt.shape
    bs, ss, _ = x_sample.shape
    depth = w_in.shape[0]
    hw, pw, mw = hgrn_lb.shape[1], pool_scale.shape[1], w_branch_mem.shape[1]
    mp, ms = bp * sp, bs * ss
    tokens = mem_prompt.shape[1]
    heads = hw // HGRN_HEAD_DIM
    pool_col, mem_col, gate_col = 4 * hw, 4 * hw + pw, 4 * hw + pw + mw

    lb_soft = jax.nn.softmax(hgrn_lb.astype(F32), axis=0)
    lb_all = jnp.cumsum(lb_soft, axis=0) - lb_soft[0:1]

    x = jnp.concatenate([x_prompt.reshape(mp, d), x_sample.reshape(ms, d)], axis=0)
    mem2 = mem_prompt.reshape(bp * tokens, d)
    h = _rmsnorm(x, norm_pre_mix[0])
    outs = [[] for _ in range(6)]
    for l in range(depth):
        kv = _proj(_rmsnorm(mem2, norm_mem[l]), w_mem_kv[l], out_dtype=F32, tm=1024)
        kv3 = kv.reshape(bp, tokens, 2 * mw)
        u = _proj(h, w_in[l], out_dtype=F32)

        o_p, st_p = _hgrn_prompt(u, lb_all[l], hgrn_out_norm[l], bp, sp, hw)
        o_s, st_s = _hgrn_sample(u, lb_all[l], hgrn_out_norm[l], state_hgrn[l], mp, bs, ss, hw)
        pool_p, buf_p = _pool(u, None, w_pool[l], pool_scale[l], pool_col, 0, bp, sp, 0, nb=1, tb=512)
        pool_s, buf_s = _pool(u, state_pool[l], w_pool[l], pool_scale[l], pool_col, mp, bs, ss, PAST_LEN,
                              nb=32, tb=ss)
        mem_p = _attend(u, kv3, kv3, 0, 1, mw, mem_col, 0, bp, sp, nb=1, tq=512)
        mem_s = _attend(u, cache_mem_k[l].reshape(bs, tokens, mw), cache_mem_v[l].reshape(bs, tokens, mw),
                        0, 0, mw, mem_col, mp, bs, ss, nb=4, tq=ss)

        merged = _merge(jnp.concatenate([o_p, o_s]), jnp.concatenate([pool_p, pool_s]),
                        jnp.concatenate([mem_p, mem_s]), w_branch_hgrn[l], w_branch_pool[l], w_branch_mem[l],
                        u, gate_col)
        x, h2 = _resid(merged, w_out[l], x, norm_post_mix[l], norm_pre_mlp[l])
        ff = _proj(h2, w_up[l], out_dtype=MXU_DTYPE, relu2=True)
        x, h = _resid(ff, w_down[l], x, norm_post_mlp[l], norm_pre_mix[l + 1] if l + 1 < depth else None)

        hd = mw // MEM_HEADS
        for lst, val in zip(outs, (st_p, buf_p, kv3[..., :mw].reshape(bp, tokens, MEM_HEADS, hd),
                                   kv3[..., mw:].reshape(bp, tokens, MEM_HEADS, hd),
                                   st_s.reshape(bs, heads, HGRN_HEAD_DIM, HGRN_HEAD_DIM), buf_s)):
            lst.append(val)

    return (x[:mp].reshape(bp, sp, d), x[mp:].reshape(bs, ss, d)) + tuple(jnp.stack(o) for o in outs)
```

```python
import functools
import math

import jax
import jax.numpy as jnp
from jax import lax
from jax.experimental import pallas as pl
from jax.experimental.pallas import tpu as pltpu

F32 = jnp.float32
MXU_DTYPE = jnp.bfloat16
EPS = 1e-6
F_FLOOR = 1e-30
NEG_BIG = -1e30
HGRN_HEAD_DIM = 128
POOL_WINDOWS = (2, 4, 8, 16)
POOL_HIST = 16
MEM_HEADS = 4
N_BRANCHES = 3
V7X_VMEM_LIMIT_BYTES = 56 * 1024 * 1024
HGRN_CHUNK = 128
HGRN_DIAG = 8


def _params(*sem):
    return pltpu.CompilerParams(dimension_semantics=sem, vmem_limit_bytes=V7X_VMEM_LIMIT_BYTES)


def _tile(n, t):
    t = min(n, t)
    assert n % t == 0, (n, t)
    return t


def _rms_scale(x):
    return lax.rsqrt(jnp.mean(x * x, axis=-1, keepdims=True) + EPS)


def _rmsnorm_kernel(x_ref, g_ref, o_ref):
    x = x_ref[...]
    o_ref[...] = (x * _rms_scale(x) * g_ref[...]).astype(o_ref.dtype)


def _rmsnorm(x, g, tm=256):
    m, d = x.shape
    tm = _tile(m, tm)
    return pl.pallas_call(
        _rmsnorm_kernel,
        out_shape=jax.ShapeDtypeStruct((m, d), MXU_DTYPE),
        grid=(m // tm,),
        in_specs=[pl.BlockSpec((tm, d), lambda i: (i, 0)), pl.BlockSpec((1, d), lambda i: (0, 0))],
        out_specs=pl.BlockSpec((tm, d), lambda i: (i, 0)),
        compiler_params=_params("parallel"),
        name="rmsnorm",
    )(x, g.reshape(1, d))


def _proj_kernel(a_ref, w_ref, *rest, relu2):
    o_ref = rest[-1]
    acc = jnp.dot(a_ref[...], w_ref[...].astype(MXU_DTYPE), preferred_element_type=F32)
    if len(rest) == 2:
        acc = acc * rest[0][...]
    if relu2:
        acc = jnp.square(jnp.maximum(acc, 0.0))
    o_ref[...] = acc.astype(o_ref.dtype)


def _proj(a, w, l, row_scale=None, *, out_dtype, relu2=False, tm=2304, tn=256):
    m, k = a.shape
    n = w.shape[2]
    tm, tn = _tile(m, tm), _tile(n, tn)
    in_specs = [pl.BlockSpec((tm, k), lambda i, j: (i, 0), pipeline_mode=pl.Buffered(1)),
                pl.BlockSpec((None, k, tn), lambda i, j: (l, 0, j))]
    args = [a, w]
    if row_scale is not None:
        in_specs.append(pl.BlockSpec((tm, 1), lambda i, j: (i, 0)))
        args.append(row_scale)
    return pl.pallas_call(
        functools.partial(_proj_kernel, relu2=relu2),
        out_shape=jax.ShapeDtypeStruct((m, n), out_dtype),
        grid=(m // tm, n // tn),
        in_specs=in_specs,
        out_specs=pl.BlockSpec((tm, tn), lambda i, j: (i, j)),
        compiler_params=_params("parallel", "arbitrary"),
        name="proj",
    )(*args)


def _merge_kernel(*refs, n_first):
    w_refs, g_refs, o_ref = refs[6:9], refs[9:12], refs[12]

    def merged(a_refs):
        tot = None
        for a_ref, w_ref, g_ref in zip(a_refs, w_refs, g_refs):
            y = jnp.dot(a_ref[...], w_ref[...].astype(MXU_DTYPE), preferred_element_type=F32)
            y = jax.nn.sigmoid(g_ref[...]) * y
            tot = y if tot is None else tot + y
        o_ref[...] = tot.astype(o_ref.dtype)

    first = pl.program_id(0) < n_first
    pl.when(first)(lambda: merged(refs[0:3]))
    pl.when(jnp.logical_not(first))(lambda: merged(refs[3:6]))


def _merge(branches_a, branches_b, w_h, w_p, w_m, l, u, gate_col, *, tm=1024, tn=512):
    ma, mb = branches_a[0].shape[0], branches_b[0].shape[0]
    d = w_h.shape[2]
    tm = _tile(mb, _tile(ma, tm))
    tn = math.gcd(_tile(d, tn), gate_col)
    gc, nd, na = gate_col // tn, d // tn, ma // tm
    once = pl.Buffered(1)
    a_spec = lambda a: pl.BlockSpec((tm, a.shape[1]), lambda i, j: (jnp.minimum(i, na - 1), 0), pipeline_mode=once)
    b_spec = lambda a: pl.BlockSpec((tm, a.shape[1]), lambda i, j: (jnp.maximum(i - na, 0), 0), pipeline_mode=once)
    w_spec = lambda w: pl.BlockSpec((None, w.shape[1], tn), lambda i, j: (l, 0, j))
    g_spec = lambda b: pl.BlockSpec((tm, tn), lambda i, j: (i, gc + b * nd + j))
    return pl.pallas_call(
        functools.partial(_merge_kernel, n_first=na),
        out_shape=jax.ShapeDtypeStruct((ma + mb, d), MXU_DTYPE),
        grid=((ma + mb) // tm, nd),
        in_specs=[a_spec(a) for a in branches_a] + [b_spec(a) for a in branches_b]
                 + [w_spec(w_h), w_spec(w_p), w_spec(w_m), g_spec(0), g_spec(1), g_spec(2)],
        out_specs=pl.BlockSpec((tm, tn), lambda i, j: (i, j)),
        compiler_params=_params("parallel", "arbitrary"),
        name="merge",
    )(*branches_a, *branches_b, w_h, w_p, w_m, u, u, u)


def _resid_kernel(a_ref, w_ref, x_ref, gpost_ref, gnext_ref, ox_ref, *rest, nk, nc, tc, want_h):
    if want_h:
        oh_ref, os_ref, acc_ref, s1_ref, s2_ref = rest
    else:
        acc_ref, s1_ref, s2_ref = rest
    s = pl.program_id(1)
    n_total = nc * tc

    def partial_products():
        a = a_ref[...]
        w = w_ref[...].astype(MXU_DTYPE)
        return [jnp.dot(a, w[:, c * tc:(c + 1) * tc], preferred_element_type=F32) for c in range(nc)]

    @pl.when(s == 0)
    def _():
        for c, y in enumerate(partial_products()):
            acc_ref[c] = y

    @pl.when(jnp.logical_and(s > 0, s < nk))
    def _():
        for c, y in enumerate(partial_products()):
            acc_ref[c] += y

    @pl.when(s == nk)
    def _():
        ss = jnp.zeros(s1_ref.shape, F32)
        for c in range(nc):
            y = acc_ref[c]
            ss = ss + jnp.sum(y * y, axis=-1, keepdims=True)
        s1_ref[...] = lax.rsqrt(ss / n_total + EPS)
        s2_ref[...] = jnp.zeros(s2_ref.shape, F32)

    @pl.when(s >= nk)
    def _():
        c = s - nk
        xn = x_ref[...] + acc_ref[c] * s1_ref[...] * gpost_ref[...]
        ox_ref[...] = xn
        if want_h:
            oh_ref[...] = (xn * gnext_ref[...]).astype(oh_ref.dtype)
            s2_ref[...] += jnp.sum(xn * xn, axis=-1, keepdims=True)

    if want_h:
        @pl.when(s == nk + nc - 1)
        def _():
            os_ref[...] = lax.rsqrt(s2_ref[...] / n_total + EPS)


def _resid(a, w, l, x, g_post, g_next, *, tm=1024, tk=512, tc=512):
    m, k = a.shape
    n = w.shape[2]
    tm, tk, tc = _tile(m, tm), _tile(k, tk), _tile(n, tc)
    nk, nc = k // tk, n // tc
    want_h = g_next is not None
    if not want_h:
        g_next = g_post
    kk = lambda s: jnp.minimum(s, nk - 1)
    cc = lambda s: jnp.maximum(s - nk, 0)
    out_shape = [jax.ShapeDtypeStruct((m, n), F32)]
    out_specs = [pl.BlockSpec((tm, tc), lambda i, s: (i, cc(s)))]
    if want_h:
        out_shape += [jax.ShapeDtypeStruct((m, n), MXU_DTYPE), jax.ShapeDtypeStruct((m, 1), F32)]
        out_specs += [pl.BlockSpec((tm, tc), lambda i, s: (i, cc(s))), pl.BlockSpec((tm, 1), lambda i, s: (i, 0))]
    outs = pl.pallas_call(
        functools.partial(_resid_kernel, nk=nk, nc=nc, tc=tc, want_h=want_h),
        out_shape=out_shape,
        grid=(m // tm, nk + nc),
        in_specs=[pl.BlockSpec((tm, tk), lambda i, s: (i, kk(s))),
                  pl.BlockSpec((None, tk, n), lambda i, s: (l, kk(s), 0)),
                  pl.BlockSpec((tm, tc), lambda i, s: (i, cc(s))),
                  pl.BlockSpec((1, tc), lambda i, s: (0, cc(s))),
                  pl.BlockSpec((1, tc), lambda i, s: (0, cc(s)))],
        out_specs=out_specs,
        scratch_shapes=[pltpu.VMEM((nc, tm, tc), F32), pltpu.VMEM((tm, 1), F32), pltpu.VMEM((tm, 1), F32)],
        compiler_params=_params("parallel", "arbitrary"),
        name="resid",
    )(a, w, x, g_post.reshape(1, n), g_next.reshape(1, n))
    return tuple(outs) if want_h else (outs[0], None, None)


def _hgrn_gates(uq, uf, lb):
    f = lb + (1.0 - lb) * jax.nn.sigmoid(uf)
    logf = jnp.log(jnp.maximum(f, F_FLOOR))
    q = jax.nn.silu(uq) * (HGRN_HEAD_DIM ** -0.5)
    return q, 1.0 - f, logf


def _hgrn_finish(o, ug, gain, dtype):
    o = o * _rms_scale(o) * gain
    return (o * jax.nn.silu(ug)).astype(dtype)


def _split3(x):
    h1 = x.astype(MXU_DTYPE)
    r = x - h1.astype(F32)
    h2 = r.astype(MXU_DTYPE)
    h3 = (r - h2.astype(F32)).astype(MXU_DTYPE)
    return h1, h2, h3


def _diag_offsets(q, k, cum, row, block):
    yield 0, jnp.sum(q * k, axis=-1, keepdims=True)
    for j in range(1, block):
        inside = (row % block) >= j
        d = jnp.where(inside, cum - pltpu.roll(cum, j, 0), NEG_BIG)
        yield j, jnp.sum(q * pltpu.roll(k, j, 0) * jnp.exp(d), axis=-1, keepdims=True)


def _hgrn_chunk(q, k, v, logf, st):
    c = q.shape[0]
    row = lax.broadcasted_iota(jnp.int32, (c, 1), 0)
    rr = lax.broadcasted_iota(jnp.int32, (c, c), 0)
    cc = lax.broadcasted_iota(jnp.int32, (c, c), 1)
    tri = (cc <= rr).astype(MXU_DTYPE)
    cum = sum(jnp.dot(tri, h, preferred_element_type=F32) for h in _split3(logf)[::-1])

    scores = jnp.zeros((c, c), F32)
    for j, rs in _diag_offsets(q, k, cum, row, HGRN_DIAG):
        scores = scores + jnp.where(cc == rr - j, rs, 0.0)
    b = HGRN_DIAG
    while b < c:
        c3 = cum.reshape(c // (2 * b), 2 * b, cum.shape[1])
        e = jnp.exp(-jnp.abs(c3 - c3[:, b - 1:b, :])).reshape(cum.shape)
        odd = ((row // b) % 2) == 1
        qt = jnp.where(odd, q * e, 0.0).astype(MXU_DTYPE)
        kt = jnp.where(odd, 0.0, k * e).astype(MXU_DTYPE)
        s = lax.dot_general(qt, kt, (((1,), (1,)), ((), ())), preferred_element_type=F32)
        scores = scores + jnp.where((rr // (2 * b)) == (cc // (2 * b)), s, 0.0)
        b *= 2

    vm = v.astype(MXU_DTYPE)
    qd = (q * jnp.exp(cum)).astype(MXU_DTYPE)
    o = lax.dot_general(qd, st.astype(MXU_DTYPE), (((1,), (1,)), ((), ())), preferred_element_type=F32)
    o = o + jnp.dot(scores.astype(MXU_DTYPE), vm, preferred_element_type=F32)
    last = cum[c - 1:c, :]
    kd = (k * jnp.exp(last - cum)).astype(MXU_DTYPE)
    st_new = st * jnp.exp(last) + jnp.dot(v.T.astype(MXU_DTYPE), kd, preferred_element_type=F32)
    return o, st_new


def _hgrn_prompt_kernel(uq_ref, uf_ref, ui_ref, ug_ref, lb_ref, gain_ref, o_ref, s_ref, st_ref, *, chunk):
    n = pl.program_id(2)

    @pl.when(n == 0)
    def _():
        st_ref[...] = jnp.zeros(st_ref.shape, F32)

    lb, gain = lb_ref[...], gain_ref[...]
    for ci in range(uq_ref.shape[0] // chunk):
        rows = pl.ds(ci * chunk, chunk)
        q, k, logf = _hgrn_gates(uq_ref[rows, :], uf_ref[rows, :], lb)
        o, st_new = _hgrn_chunk(q, k, ui_ref[rows, :], logf, st_ref[...])
        st_ref[...] = st_new
        o_ref[rows, :] = _hgrn_finish(o, ug_ref[rows, :], gain, o_ref.dtype)

    @pl.when(n == pl.num_programs(2) - 1)
    def _():
        s_ref[...] = st_ref[...].T


def _hgrn_prompt(u, lb, gain, batch, seq, width, *, tb=512):
    dk = HGRN_HEAD_DIM
    heads = width // dk
    chunk = min(HGRN_CHUNK, seq)
    tb = _tile(seq, tb)
    assert tb % chunk == 0
    nt = seq // tb
    u_spec = lambda c: pl.BlockSpec((tb, dk), lambda b, h, n: (b * nt + n, c * heads + h))
    vec_spec = pl.BlockSpec((1, dk), lambda b, h, n: (0, h))
    return pl.pallas_call(
        functools.partial(_hgrn_prompt_kernel, chunk=chunk),
        out_shape=[jax.ShapeDtypeStruct((batch * seq, width), MXU_DTYPE),
                   jax.ShapeDtypeStruct((batch, heads, dk, dk), F32)],
        grid=(batch, heads, nt),
        in_specs=[u_spec(0), u_spec(1), u_spec(2), u_spec(3), vec_spec, vec_spec],
        out_specs=[pl.BlockSpec((tb, dk), lambda b, h, n: (b * nt + n, h)),
                   pl.BlockSpec((None, None, dk, dk), lambda b, h, n: (b, h, 0, 0))],
        scratch_shapes=[pltpu.VMEM((dk, dk), F32)],
        compiler_params=_params("parallel", "parallel", "arbitrary"),
        name="hgrn_prompt",
    )(u, u, u, u, lb.reshape(1, width), gain.reshape(1, width))


def _hgrn_sample_kernel(uq_ref, uf_ref, ui_ref, ug_ref, lb_ref, gain_ref, s0_ref, *rest, t):
    o_ref, s_ref = rest[-2:]
    n_prev = s_ref.shape[0] - 1
    for i in range(n_prev):
        s_ref[i] = rest[0][i]
    nb = s0_ref.shape[0]
    width = uq_ref.shape[1]
    dk = HGRN_HEAD_DIM
    heads = width // dk
    pad = dk - 2 * t
    assert pad >= 0
    lb = lb_ref[...]
    row = lax.broadcasted_iota(jnp.int32, (t, 1), 0)
    outs = [[] for _ in range(heads)]
    for sq in range(nb):
        rows = pl.ds(sq * t, t)
        q, k, logf = _hgrn_gates(uq_ref[rows, :], uf_ref[rows, :], lb)
        v = ui_ref[rows, :]
        cum = logf
        sh = 1
        while sh < t:
            cum = cum + jnp.where(row >= sh, pltpu.roll(cum, sh, 0), 0.0)
            sh *= 2
        last = cum[t - 1:t, :]
        qd = q * jnp.exp(cum)
        kd = k * jnp.exp(last - cum)
        dec = jnp.exp(last)
        for h in range(heads):
            cols = slice(h * dk, (h + 1) * dk)
            vh = v[:, cols]
            s0 = s0_ref[sq, h]
            o = jnp.dot(qd[:, cols].astype(MXU_DTYPE), s0.astype(MXU_DTYPE), preferred_element_type=F32)
            for j, rs in _diag_offsets(q[:, cols], k[:, cols], cum[:, cols], row, t):
                o = o + rs * (vh if j == 0 else pltpu.roll(vh, j, 0))
            outs[h].append(o)
            stack_t = jnp.concatenate([kd[:, cols], jnp.broadcast_to(dec[:, cols], (t, dk)),
                                       jnp.zeros((pad, dk), F32)], axis=0).T
            v_pad = jnp.concatenate([vh, jnp.zeros((dk - t, dk), F32)], axis=0)
            s_ref[n_prev, sq, h] = stack_t[:, t:t + 1] * s0 + jnp.dot(stack_t.astype(MXU_DTYPE), v_pad.astype(MXU_DTYPE),
                                                               preferred_element_type=F32)
    for h in range(heads):
        cols = slice(h * dk, (h + 1) * dk)
        o = jnp.concatenate(outs[h], axis=0)
        o_ref[:, cols] = _hgrn_finish(o, ug_ref[:, cols], gain_ref[:, cols], o_ref.dtype)


def _hgrn_sample(u, lb, gain, s0, l, prev, row0, batch, t, width, *, nb=2):
    dk = HGRN_HEAD_DIM
    heads = width // dk
    nb = _tile(batch, nb)
    assert row0 % (nb * t) == 0 and (prev is None) == (l == 0)
    r0 = row0 // (nb * t)
    u_spec = lambda c: pl.BlockSpec((nb * t, width), lambda b: (r0 + b, c))
    vec_spec = pl.BlockSpec((1, width), lambda b: (0, 0))
    stack_spec = lambda n: pl.BlockSpec((n, nb, heads, dk, dk), lambda b: (0, b, 0, 0, 0))
    in_specs = [u_spec(0), u_spec(1), u_spec(2), u_spec(3), vec_spec, vec_spec,
                pl.BlockSpec((None, nb, heads, dk, dk), lambda b: (l, b, 0, 0, 0))]
    args = [u, u, u, u, lb.reshape(1, width), gain.reshape(1, width), s0]
    if prev is not None:
        in_specs.append(stack_spec(l))
        args.append(prev)
    return pl.pallas_call(
        functools.partial(_hgrn_sample_kernel, t=t),
        out_shape=[jax.ShapeDtypeStruct((batch * t, width), MXU_DTYPE),
                   jax.ShapeDtypeStruct((l + 1, batch, heads, dk, dk), F32)],
        grid=(batch // nb,),
        in_specs=in_specs,
        out_specs=[pl.BlockSpec((nb * t, width), lambda b: (b, 0)), stack_spec(l + 1)],
        compiler_params=_params("parallel"),
        name="hgrn_sample",
    )(*args)


def _pool_kernel(p_ref, buf_ref, w_ref, scale_ref, o_ref, nbuf_ref, ext_ref, *, n_past):
    n = pl.program_id(1)
    nb, tb, width = p_ref.shape
    gw = width // len(POOL_WINDOWS)
    hist = POOL_HIST

    @pl.when(n == 0)
    def _():
        ext_ref[:, 0:hist, :] = buf_ref[...]

    @pl.when(n > 0)
    def _():
        ext_ref[:, 0:hist, :] = ext_ref[:, tb:tb + hist, :]

    ext_ref[:, hist:hist + tb, :] = p_ref[...]
    pos = n_past + n * tb + lax.broadcasted_iota(jnp.int32, (1, tb, 1), 1)
    for g, win in enumerate(POOL_WINDOWS):
        cols = slice(g * gw, (g + 1) * gw)
        tot = ext_ref[:, hist:hist + tb, cols]
        x = tot
        for i in range(1, win):
            tot = tot + ext_ref[:, hist - i:hist - i + tb, cols]
        count = jnp.minimum(pos + 1, win).astype(F32)
        pooled = (tot / count - x).reshape(nb * tb, gw)
        mixed = jnp.dot(pooled.astype(MXU_DTYPE), w_ref[g].astype(MXU_DTYPE), preferred_element_type=F32)
        o_ref[:, cols] = (mixed * scale_ref[:, cols]).astype(o_ref.dtype)

    @pl.when(n == pl.num_programs(1) - 1)
    def _():
        nbuf_ref[...] = ext_ref[:, tb:tb + hist, :]


def _pool(u, buf, w_grp, l, scale, col, row0, batch, seq, n_past, *, nb, tb):
    _, groups, gw, _ = w_grp.shape
    width = groups * gw
    hist = POOL_HIST
    nb, tb = _tile(batch, nb), _tile(seq, tb)
    nt = seq // tb
    assert col % width == 0 and row0 % (nb * tb) == 0 and (nb == 1 or nt == 1) and tb % 8 == 0
    if buf is None:
        buf = jnp.zeros((batch, hist, width), F32)
    else:
        buf = jnp.pad(buf, ((0, 0), (1, 0), (0, 0)))
    u3 = u.reshape(u.shape[0] // tb, tb, u.shape[1])
    r0 = row0 // (nb * tb)
    mixed, nbuf = pl.pallas_call(
        functools.partial(_pool_kernel, n_past=n_past),
        out_shape=[jax.ShapeDtypeStruct((batch * seq, width), MXU_DTYPE),
                   jax.ShapeDtypeStruct((batch, hist, width), F32)],
        grid=(batch // nb, nt),
        in_specs=[pl.BlockSpec((nb, tb, width), lambda b, n: (r0 + b * nt + n, 0, col // width)),
                  pl.BlockSpec((nb, hist, width), lambda b, n: (b, 0, 0)),
                  pl.BlockSpec((None, groups, gw, gw), lambda b, n: (l, 0, 0, 0)),
                  pl.BlockSpec((1, width), lambda b, n: (0, 0))],
        out_specs=[pl.BlockSpec((nb * tb, width), lambda b, n: (b * nt + n, 0)),
                   pl.BlockSpec((nb, hist, width), lambda b, n: (b, 0, 0))],
        scratch_shapes=[pltpu.VMEM((nb, hist + tb, width), F32)],
        compiler_params=_params("parallel", "arbitrary"),
        name="pool",
    )(u3, buf, w_grp, scale.reshape(1, width))
    return mixed, nbuf[:, 1:, :]


def _attend_kernel(q_ref, k_ref, v_ref, o_ref, *, tq, hd):
    nb = k_ref.shape[0]
    for b in range(nb):
        rows = slice(b * tq, (b + 1) * tq)
        for h in range(q_ref.shape[1] // hd):
            cols = slice(h * hd, (h + 1) * hd)
            k, v = k_ref[b, :, cols], v_ref[b, :, cols]
            q = q_ref[rows, cols].astype(MXU_DTYPE)
            s = lax.dot_general(q, k.astype(MXU_DTYPE), (((1,), (1,)), ((), ())),
                                preferred_element_type=F32) * (hd ** -0.5)
            e = jnp.exp(s - jnp.max(s, axis=-1, keepdims=True))
            a = e / jnp.sum(e, axis=-1, keepdims=True)
            o_ref[rows, cols] = jnp.dot(a.astype(MXU_DTYPE), v.astype(MXU_DTYPE),
                                        preferred_element_type=F32).astype(o_ref.dtype)


def _attend(u, k_arr, v_arr, k_spec, v_spec, hd, hps, col, row0, batch, seq, *, nb, tq):
    nb, tq = _tile(batch, nb), _tile(seq, tq)
    nt = seq // tq
    wq = hps * hd
    assert col % wq == 0 and row0 % (nb * tq) == 0 and (nb == 1 or nt == 1) and MEM_HEADS % hps == 0
    r0, c0 = row0 // (nb * tq), col // wq
    return pl.pallas_call(
        functools.partial(_attend_kernel, tq=tq, hd=hd),
        out_shape=jax.ShapeDtypeStruct((batch * seq, MEM_HEADS * hd), MXU_DTYPE),
        grid=(batch // nb, nt, MEM_HEADS // hps),
        in_specs=[pl.BlockSpec((nb * tq, wq), lambda b, n, h: (r0 + b * nt + n, c0 + h)), k_spec(nb), v_spec(nb)],
        out_specs=pl.BlockSpec((nb * tq, wq), lambda b, n, h: (b * nt + n, h)),
        compiler_params=_params("parallel", "arbitrary", "arbitrary"),
        name="attend",
    )(u, k_arr, v_arr)


PAST_LEN = 16384


def kernel(x_prompt, x_sample, state_hgrn, state_pool, cache_mem_k, cache_mem_v, mem_prompt, norm_pre_mix, norm_post_mix, norm_pre_mlp, norm_post_mlp, norm_mem, w_in, hgrn_lb, hgrn_out_norm, w_pool, pool_scale, w_mem_kv, w_branch_hgrn, w_branch_pool, w_branch_mem, w_out, w_up, w_down):
    bp, sp, d = x_prompt.shape
    bs, ss, _ = x_sample.shape
    depth = w_in.shape[0]
    hw, pw, mw = hgrn_lb.shape[1], pool_scale.shape[1], w_branch_mem.shape[1]
    mp, ms = bp * sp, bs * ss
    tokens = mem_prompt.shape[1]
    heads = hw // HGRN_HEAD_DIM
    pool_col, mem_col, gate_col = 4 * hw, 4 * hw + pw, 4 * hw + pw + mw

    lb_soft = jax.nn.softmax(hgrn_lb.astype(F32), axis=0)
    lb_all = jnp.cumsum(lb_soft, axis=0) - lb_soft[0:1]

    x = jnp.concatenate([x_prompt.reshape(mp, d), x_sample.reshape(ms, d)], axis=0)
    mem2 = mem_prompt.reshape(bp * tokens, d)
    h, h_scale = _rmsnorm(x, norm_pre_mix[0]), None
    hd = mw // MEM_HEADS
    outs = [[] for _ in range(5)]
    st_s = None
    for l in range(depth):
        kv = _proj(_rmsnorm(mem2, norm_mem[l]), w_mem_kv, l, out_dtype=F32, tm=1024)
        kv3 = kv.reshape(bp, tokens, 2 * mw)
        u = _proj(h, w_in, l, h_scale, out_dtype=F32)

        o_p, st_p = _hgrn_prompt(u, lb_all[l], hgrn_out_norm[l], bp, sp, hw)
        o_s, st_s = _hgrn_sample(u, lb_all[l], hgrn_out_norm[l], state_hgrn, l, st_s, mp, bs, ss, hw)
        pool_p, buf_p = _pool(u, None, w_pool, l, pool_scale[l], pool_col, 0, bp, sp, 0, nb=1, tb=512)
        pool_s, buf_s = _pool(u, state_pool[l], w_pool, l, pool_scale[l], pool_col, mp, bs, ss, PAST_LEN,
                              nb=32, tb=ss)
        kv_spec = lambda c: lambda nb: pl.BlockSpec((nb, tokens, mw), lambda b, n, h: (b, 0, c))
        mem_p = _attend(u, kv3, kv3, kv_spec(0), kv_spec(1), hd, MEM_HEADS, mem_col, 0, bp, sp, nb=1, tq=512)
        mem_s = _attend(u, cache_mem_k[l].reshape(bs, tokens, mw), cache_mem_v[l].reshape(bs, tokens, mw),
                        kv_spec(0), kv_spec(0), hd, MEM_HEADS, mem_col, mp, bs, ss, nb=4, tq=ss)

        merged = _merge((o_p, pool_p, mem_p), (o_s, pool_s, mem_s), w_branch_hgrn, w_branch_pool, w_branch_mem, l,
                        u, gate_col)
        x, h2, h2_scale = _resid(merged, w_out, l, x, norm_post_mix[l], norm_pre_mlp[l])
        ff = _proj(h2, w_up, l, h2_scale, out_dtype=MXU_DTYPE, relu2=True)
        x, h, h_scale = _resid(ff, w_down, l, x, norm_post_mlp[l], norm_pre_mix[l + 1] if l + 1 < depth else None)

        for lst, val in zip(outs, (st_p, buf_p, kv3[..., :mw].reshape(bp, tokens, MEM_HEADS, hd),
                                   kv3[..., mw:].reshape(bp, tokens, MEM_HEADS, hd), buf_s)):
            lst.append(val)

    stacked = [jnp.stack(o) for o in outs]
    return (x[:mp].reshape(bp, sp, d), x[mp:].reshape(bs, ss, d), *stacked[:4], st_s, stacked[4])
```

```python
import functools
import math

import jax
import jax.numpy as jnp
from jax import lax
from jax.experimental import pallas as pl
from jax.experimental.pallas import tpu as pltpu

F32 = jnp.float32
MXU_DTYPE = jnp.bfloat16
EPS = 1e-6
F_FLOOR = 1e-30
NEG_BIG = -1e30
HGRN_HEAD_DIM = 128
POOL_WINDOWS = (2, 4, 8, 16)
POOL_HIST = 16
MEM_HEADS = 4
N_BRANCHES = 3
V7X_VMEM_LIMIT_BYTES = 56 * 1024 * 1024
HGRN_CHUNK = 128
HGRN_DIAG = 8


def _params(*sem):
    return pltpu.CompilerParams(dimension_semantics=sem, vmem_limit_bytes=V7X_VMEM_LIMIT_BYTES)


def _tile(n, t, quantum=1):
    if n <= t:
        return n
    t -= t % quantum
    while n % t:
        t -= quantum
    assert t > 0, (n, quantum)
    return t


def _rms_scale(x):
    return lax.rsqrt(jnp.mean(x * x, axis=-1, keepdims=True) + EPS)


def _rmsnorm_kernel(x_ref, g_ref, o_ref):
    x = x_ref[...]
    o_ref[...] = (x * _rms_scale(x) * g_ref[...]).astype(o_ref.dtype)


def _rmsnorm(x, g, tm=256):
    m, d = x.shape
    tm = _tile(m, tm)
    return pl.pallas_call(
        _rmsnorm_kernel,
        out_shape=jax.ShapeDtypeStruct((m, d), MXU_DTYPE),
        grid=(m // tm,),
        in_specs=[pl.BlockSpec((tm, d), lambda i: (i, 0)), pl.BlockSpec((1, d), lambda i: (0, 0))],
        out_specs=pl.BlockSpec((tm, d), lambda i: (i, 0)),
        compiler_params=_params("parallel"),
        name="rmsnorm",
    )(x, g.reshape(1, d))


def _proj_kernel(a_ref, w_ref, *rest, relu2):
    o_ref = rest[-1]
    acc = jnp.dot(a_ref[...], w_ref[...].astype(MXU_DTYPE), preferred_element_type=F32)
    if len(rest) == 2:
        acc = acc * rest[0][...]
    if relu2:
        acc = jnp.square(jnp.maximum(acc, 0.0))
    o_ref[...] = acc.astype(o_ref.dtype)


def _proj(a, w, l, row_scale=None, *, out_dtype, relu2=False, tm=1536, tn=512):
    m, k = a.shape
    n = w.shape[2]
    tm, tn = _tile(m, tm), _tile(n, tn, LANES)
    in_specs = [pl.BlockSpec((tm, k), lambda i, j: (i, 0), pipeline_mode=pl.Buffered(1)),
                pl.BlockSpec((None, k, tn), lambda i, j: (l, 0, j))]
    args = [a, w]
    if row_scale is not None:
        in_specs.append(pl.BlockSpec((tm, 1), lambda i, j: (i, 0)))
        args.append(row_scale)
    return pl.pallas_call(
        functools.partial(_proj_kernel, relu2=relu2),
        out_shape=jax.ShapeDtypeStruct((m, n), out_dtype),
        grid=(m // tm, n // tn),
        in_specs=in_specs,
        out_specs=pl.BlockSpec((tm, tn), lambda i, j: (i, j)),
        compiler_params=_params("parallel", "arbitrary"),
        name="proj",
    )(*args)


def _merge_kernel(*refs, n_first):
    w_refs, g_refs, o_ref = refs[6:9], refs[9:12], refs[12]

    def merged(a_refs):
        tot = None
        for a_ref, w_ref, g_ref in zip(a_refs, w_refs, g_refs):
            y = jnp.dot(a_ref[...], w_ref[...].astype(MXU_DTYPE), preferred_element_type=F32)
            y = jax.nn.sigmoid(g_ref[...]) * y
            tot = y if tot is None else tot + y
        o_ref[...] = tot.astype(o_ref.dtype)

    first = pl.program_id(0) < n_first
    pl.when(first)(lambda: merged(refs[0:3]))
    pl.when(jnp.logical_not(first))(lambda: merged(refs[3:6]))


def _merge(branches_a, branches_b, w_h, w_p, w_m, l, u, gate_col, *, tm=1024, tn=512):
    ma, mb = branches_a[0].shape[0], branches_b[0].shape[0]
    d = w_h.shape[2]
    tm = _tile(mb, _tile(ma, tm))
    tn = math.gcd(_tile(d, tn, LANES), gate_col)
    gc, nd, na = gate_col // tn, d // tn, ma // tm
    once = pl.Buffered(1)
    a_spec = lambda a: pl.BlockSpec((tm, a.shape[1]), lambda i, j: (jnp.minimum(i, na - 1), 0), pipeline_mode=once)
    b_spec = lambda a: pl.BlockSpec((tm, a.shape[1]), lambda i, j: (jnp.maximum(i - na, 0), 0), pipeline_mode=once)
    w_spec = lambda w: pl.BlockSpec((None, w.shape[1], tn), lambda i, j: (l, 0, j))
    g_spec = lambda b: pl.BlockSpec((tm, tn), lambda i, j: (i, gc + b * nd + j))
    return pl.pallas_call(
        functools.partial(_merge_kernel, n_first=na),
        out_shape=jax.ShapeDtypeStruct((ma + mb, d), MXU_DTYPE),
        grid=((ma + mb) // tm, nd),
        in_specs=[a_spec(a) for a in branches_a] + [b_spec(a) for a in branches_b]
                 + [w_spec(w_h), w_spec(w_p), w_spec(w_m), g_spec(0), g_spec(1), g_spec(2)],
        out_specs=pl.BlockSpec((tm, tn), lambda i, j: (i, j)),
        compiler_params=_params("parallel", "arbitrary"),
        name="merge",
    )(*branches_a, *branches_b, w_h, w_p, w_m, u, u, u)


def _resid_kernel(a_ref, w_ref, x_ref, gpost_ref, gnext_ref, ox_ref, *rest, nk, nc, tc, want_h):
    if want_h:
        oh_ref, os_ref, acc_ref, s1_ref, s2_ref = rest
    else:
        acc_ref, s1_ref, s2_ref = rest
    s = pl.program_id(1)
    n_total = nc * tc

    def partial_products():
        a = a_ref[...]
        w = w_ref[...].astype(MXU_DTYPE)
        return [jnp.dot(a, w[:, c * tc:(c + 1) * tc], preferred_element_type=F32) for c in range(nc)]

    @pl.when(s == 0)
    def _():
        for c, y in enumerate(partial_products()):
            acc_ref[c] = y

    @pl.when(jnp.logical_and(s > 0, s < nk))
    def _():
        for c, y in enumerate(partial_products()):
            acc_ref[c] += y

    @pl.when(s == nk)
    def _():
        ss = jnp.zeros(s1_ref.shape, F32)
        for c in range(nc):
            y = acc_ref[c]
            ss = ss + jnp.sum(y * y, axis=-1, keepdims=True)
        s1_ref[...] = lax.rsqrt(ss / n_total + EPS)
        s2_ref[...] = jnp.zeros(s2_ref.shape, F32)

    @pl.when(s >= nk)
    def _():
        c = s - nk
        xn = x_ref[...] + acc_ref[c] * s1_ref[...] * gpost_ref[...]
        ox_ref[...] = xn
        if want_h:
            oh_ref[...] = (xn * gnext_ref[...]).astype(oh_ref.dtype)
            s2_ref[...] += jnp.sum(xn * xn, axis=-1, keepdims=True)

    if want_h:
        @pl.when(s == nk + nc - 1)
        def _():
            os_ref[...] = lax.rsqrt(s2_ref[...] / n_total + EPS)


def _resid(a, w, l, x, g_post, g_next, *, tm=1024, tk=512, tc=512):
    m, k = a.shape
    n = w.shape[2]
    tm, tk, tc = _tile(m, tm), _tile(k, tk, LANES), _tile(n, tc, LANES)
    nk, nc = k // tk, n // tc
    want_h = g_next is not None
    if not want_h:
        g_next = g_post
    kk = lambda s: jnp.minimum(s, nk - 1)
    cc = lambda s: jnp.maximum(s - nk, 0)
    out_shape = [jax.ShapeDtypeStruct((m, n), F32)]
    out_specs = [pl.BlockSpec((tm, tc), lambda i, s: (i, cc(s)))]
    if want_h:
        out_shape += [jax.ShapeDtypeStruct((m, n), MXU_DTYPE), jax.ShapeDtypeStruct((m, 1), F32)]
        out_specs += [pl.BlockSpec((tm, tc), lambda i, s: (i, cc(s))), pl.BlockSpec((tm, 1), lambda i, s: (i, 0))]
    outs = pl.pallas_call(
        functools.partial(_resid_kernel, nk=nk, nc=nc, tc=tc, want_h=want_h),
        out_shape=out_shape,
        grid=(m // tm, nk + nc),
        in_specs=[pl.BlockSpec((tm, tk), lambda i, s: (i, kk(s))),
                  pl.BlockSpec((None, tk, n), lambda i, s: (l, kk(s), 0)),
                  pl.BlockSpec((tm, tc), lambda i, s: (i, cc(s))),
                  pl.BlockSpec((1, tc), lambda i, s: (0, cc(s))),
                  pl.BlockSpec((1, tc), lambda i, s: (0, cc(s)))],
        out_specs=out_specs,
        scratch_shapes=[pltpu.VMEM((nc, tm, tc), F32), pltpu.VMEM((tm, 1), F32), pltpu.VMEM((tm, 1), F32)],
        compiler_params=_params("parallel", "arbitrary"),
        name="resid",
    )(a, w, x, g_post.reshape(1, n), g_next.reshape(1, n))
    return tuple(outs) if want_h else (outs[0], None, None)


def _hgrn_gates(uq, uf, lb):
    f = lb + (1.0 - lb) * jax.nn.sigmoid(uf)
    logf = jnp.log(jnp.maximum(f, F_FLOOR))
    q = jax.nn.silu(uq) * (HGRN_HEAD_DIM ** -0.5)
    return q, 1.0 - f, logf


def _hgrn_finish(o, ug, gain, dtype):
    o = o * _rms_scale(o) * gain
    return (o * jax.nn.silu(ug)).astype(dtype)


def _split3(x):
    h1 = x.astype(MXU_DTYPE)
    r = x - h1.astype(F32)
    h2 = r.astype(MXU_DTYPE)
    h3 = (r - h2.astype(F32)).astype(MXU_DTYPE)
    return h1, h2, h3


def _diag_offsets(q, k, cum, row, block):
    yield 0, jnp.sum(q * k, axis=-1, keepdims=True)
    for j in range(1, block):
        inside = (row % block) >= j
        d = jnp.where(inside, cum - pltpu.roll(cum, j, 0), NEG_BIG)
        yield j, jnp.sum(q * pltpu.roll(k, j, 0) * jnp.exp(d), axis=-1, keepdims=True)


def _diag_blocks(q, k, v, cum, tmp_ref):
    g = HGRN_DIAG
    n = q.shape[0] // g
    for i, a in enumerate((q, k, v, cum)):
        tmp_ref[i] = a
    views = [[tmp_ref[i, pl.ds(r, n, stride=g), :] for r in range(g)] for i in range(4)]
    qs, ks, vs, cs = views
    for r in range(g):
        acc = jnp.sum(qs[r] * ks[r], axis=-1, keepdims=True) * vs[r]
        for r2 in range(r):
            p = qs[r] * ks[r2] * jnp.exp(cs[r] - cs[r2])
            acc = acc + jnp.sum(p, axis=-1, keepdims=True) * vs[r2]
        tmp_ref[4, pl.ds(r, n, stride=g), :] = acc
    return tmp_ref[4]


def _hgrn_chunk(q, k, v, logf, st, tmp_ref):
    c = q.shape[0]
    row = lax.broadcasted_iota(jnp.int32, (c, 1), 0)
    rr = lax.broadcasted_iota(jnp.int32, (c, c), 0)
    cc = lax.broadcasted_iota(jnp.int32, (c, c), 1)
    tri = (cc <= rr).astype(MXU_DTYPE)
    cum = sum(jnp.dot(tri, h, preferred_element_type=F32) for h in _split3(logf)[::-1])

    o_diag = _diag_blocks(q, k, v, cum, tmp_ref)
    scores = jnp.zeros((c, c), F32)
    b = HGRN_DIAG
    while b < c:
        c3 = cum.reshape(c // (2 * b), 2 * b, cum.shape[1])
        e = jnp.exp(-jnp.abs(c3 - c3[:, b - 1:b, :])).reshape(cum.shape)
        odd = ((row // b) % 2) == 1
        qt = jnp.where(odd, q * e, 0.0).astype(MXU_DTYPE)
        kt = jnp.where(odd, 0.0, k * e).astype(MXU_DTYPE)
        s = lax.dot_general(qt, kt, (((1,), (1,)), ((), ())), preferred_element_type=F32)
        scores = scores + jnp.where((rr // (2 * b)) == (cc // (2 * b)), s, 0.0)
        b *= 2

    vm = v.astype(MXU_DTYPE)
    qd = (q * jnp.exp(cum)).astype(MXU_DTYPE)
    o = lax.dot_general(qd, st.astype(MXU_DTYPE), (((1,), (1,)), ((), ())), preferred_element_type=F32)
    o = o + jnp.dot(scores.astype(MXU_DTYPE), vm, preferred_element_type=F32) + o_diag
    last = cum[c - 1:c, :]
    kd = (k * jnp.exp(last - cum)).astype(MXU_DTYPE)
    st_new = st * jnp.exp(last) + jnp.dot(v.T.astype(MXU_DTYPE), kd, preferred_element_type=F32)
    return o, st_new


def _hgrn_prompt_kernel(uq_ref, uf_ref, ui_ref, ug_ref, lb_ref, gain_ref, o_ref, s_ref, st_ref, tmp_ref, *, chunk):
    n = pl.program_id(2)

    @pl.when(n == 0)
    def _():
        st_ref[...] = jnp.zeros(st_ref.shape, F32)

    lb, gain = lb_ref[...], gain_ref[...]
    for ci in range(uq_ref.shape[0] // chunk):
        rows = pl.ds(ci * chunk, chunk)
        q, k, logf = _hgrn_gates(uq_ref[rows, :], uf_ref[rows, :], lb)
        o, st_new = _hgrn_chunk(q, k, ui_ref[rows, :], logf, st_ref[...], tmp_ref)
        st_ref[...] = st_new
        o_ref[rows, :] = _hgrn_finish(o, ug_ref[rows, :], gain, o_ref.dtype)

    @pl.when(n == pl.num_programs(2) - 1)
    def _():
        s_ref[...] = st_ref[...].T


def _hgrn_prompt(u, lb, gain, batch, seq, width, *, tb=512):
    dk = HGRN_HEAD_DIM
    heads = width // dk
    chunk = min(HGRN_CHUNK, seq)
    tb = _tile(seq, tb)
    assert tb % chunk == 0
    nt = seq // tb
    u_spec = lambda c: pl.BlockSpec((tb, dk), lambda b, h, n: (b * nt + n, c * heads + h))
    vec_spec = pl.BlockSpec((1, dk), lambda b, h, n: (0, h))
    return pl.pallas_call(
        functools.partial(_hgrn_prompt_kernel, chunk=chunk),
        out_shape=[jax.ShapeDtypeStruct((batch * seq, width), MXU_DTYPE),
                   jax.ShapeDtypeStruct((batch, heads, dk, dk), F32)],
        grid=(batch, heads, nt),
        in_specs=[u_spec(0), u_spec(1), u_spec(2), u_spec(3), vec_spec, vec_spec],
        out_specs=[pl.BlockSpec((tb, dk), lambda b, h, n: (b * nt + n, h)),
                   pl.BlockSpec((None, None, dk, dk), lambda b, h, n: (b, h, 0, 0))],
        scratch_shapes=[pltpu.VMEM((dk, dk), F32), pltpu.VMEM((5, chunk, dk), F32)],
        compiler_params=_params("parallel", "parallel", "arbitrary"),
        name="hgrn_prompt",
    )(u, u, u, u, lb.reshape(1, width), gain.reshape(1, width))


def _hgrn_sample_kernel(uq_ref, uf_ref, ui_ref, ug_ref, lb_ref, gain_ref, s0_ref, *rest, t):
    o_ref, s_ref = rest[-2:]
    n_prev = s_ref.shape[0] - 1
    for i in range(n_prev):
        s_ref[i] = rest[0][i]
    nb = s0_ref.shape[0]
    width = uq_ref.shape[1]
    dk = HGRN_HEAD_DIM
    heads = width // dk
    pad = dk - 2 * t
    assert pad >= 0
    lb = lb_ref[...]
    row = lax.broadcasted_iota(jnp.int32, (t, 1), 0)
    outs = [[] for _ in range(heads)]
    for sq in range(nb):
        rows = pl.ds(sq * t, t)
        q, k, logf = _hgrn_gates(uq_ref[rows, :], uf_ref[rows, :], lb)
        v = ui_ref[rows, :]
        cum = logf
        sh = 1
        while sh < t:
            cum = cum + jnp.where(row >= sh, pltpu.roll(cum, sh, 0), 0.0)
            sh *= 2
        last = cum[t - 1:t, :]
        qd = q * jnp.exp(cum)
        kd = k * jnp.exp(last - cum)
        dec = jnp.exp(last)
        for h in range(heads):
            cols = slice(h * dk, (h + 1) * dk)
            vh = v[:, cols]
            s0 = s0_ref[sq, h]
            o = jnp.dot(qd[:, cols].astype(MXU_DTYPE), s0.astype(MXU_DTYPE), preferred_element_type=F32)
            for j, rs in _diag_offsets(q[:, cols], k[:, cols], cum[:, cols], row, t):
                o = o + rs * (vh if j == 0 else pltpu.roll(vh, j, 0))
            outs[h].append(o)
            stack_t = jnp.concatenate([kd[:, cols], jnp.broadcast_to(dec[:, cols], (t, dk)),
                                       jnp.zeros((pad, dk), F32)], axis=0).T
            v_pad = jnp.concatenate([vh, jnp.zeros((dk - t, dk), F32)], axis=0)
            s_ref[n_prev, sq, h] = stack_t[:, t:t + 1] * s0 + jnp.dot(stack_t.astype(MXU_DTYPE), v_pad.astype(MXU_DTYPE),
                                                               preferred_element_type=F32)
    for h in range(heads):
        cols = slice(h * dk, (h + 1) * dk)
        o = jnp.concatenate(outs[h], axis=0)
        o_ref[:, cols] = _hgrn_finish(o, ug_ref[:, cols], gain_ref[:, cols], o_ref.dtype)


def _hgrn_sample(u, lb, gain, s0, l, prev, row0, batch, t, width, *, nb=2):
    dk = HGRN_HEAD_DIM
    heads = width // dk
    nb = _tile(batch, nb)
    assert row0 % (nb * t) == 0 and (prev is None) == (l == 0)
    r0 = row0 // (nb * t)
    u_spec = lambda c: pl.BlockSpec((nb * t, width), lambda b: (r0 + b, c))
    vec_spec = pl.BlockSpec((1, width), lambda b: (0, 0))
    stack_spec = lambda n: pl.BlockSpec((n, nb, heads, dk, dk), lambda b: (0, b, 0, 0, 0))
    in_specs = [u_spec(0), u_spec(1), u_spec(2), u_spec(3), vec_spec, vec_spec,
                pl.BlockSpec((None, nb, heads, dk, dk), lambda b: (l, b, 0, 0, 0))]
    args = [u, u, u, u, lb.reshape(1, width), gain.reshape(1, width), s0]
    if prev is not None:
        in_specs.append(stack_spec(l))
        args.append(prev)
    return pl.pallas_call(
        functools.partial(_hgrn_sample_kernel, t=t),
        out_shape=[jax.ShapeDtypeStruct((batch * t, width), MXU_DTYPE),
                   jax.ShapeDtypeStruct((l + 1, batch, heads, dk, dk), F32)],
        grid=(batch // nb,),
        in_specs=in_specs,
        out_specs=[pl.BlockSpec((nb * t, width), lambda b: (b, 0)), stack_spec(l + 1)],
        compiler_params=_params("parallel"),
        name="hgrn_sample",
    )(*args)


def _pool_kernel(p_ref, buf_ref, w_ref, scale_ref, o_ref, nbuf_ref, ext_ref, *, n_past):
    n = pl.program_id(1)
    nb, tb, width = p_ref.shape
    gw = width // len(POOL_WINDOWS)
    hist = POOL_HIST

    @pl.when(n == 0)
    def _():
        ext_ref[:, 0:hist, :] = buf_ref[...]

    @pl.when(n > 0)
    def _():
        ext_ref[:, 0:hist, :] = ext_ref[:, tb:tb + hist, :]

    ext_ref[:, hist:hist + tb, :] = p_ref[...]
    pos = n_past + n * tb + lax.broadcasted_iota(jnp.int32, (1, tb, 1), 1)
    for g, win in enumerate(POOL_WINDOWS):
        cols = slice(g * gw, (g + 1) * gw)
        tot = ext_ref[:, hist:hist + tb, cols]
        x = tot
        for i in range(1, win):
            tot = tot + ext_ref[:, hist - i:hist - i + tb, cols]
        count = jnp.minimum(pos + 1, win).astype(F32)
        pooled = (tot / count - x).reshape(nb * tb, gw)
        mixed = jnp.dot(pooled.astype(MXU_DTYPE), w_ref[g].astype(MXU_DTYPE), preferred_element_type=F32)
        o_ref[:, cols] = (mixed * scale_ref[:, cols]).astype(o_ref.dtype)

    @pl.when(n == pl.num_programs(1) - 1)
    def _():
        nbuf_ref[...] = ext_ref[:, tb:tb + hist, :]


def _pool(u, buf, w_grp, l, scale, col, row0, batch, seq, n_past, *, nb, tb):
    _, groups, gw, _ = w_grp.shape
    width = groups * gw
    hist = POOL_HIST
    nb, tb = _tile(batch, nb), _tile(seq, tb)
    nt = seq // tb
    assert col % width == 0 and row0 % (nb * tb) == 0 and (nb == 1 or nt == 1) and tb % 8 == 0
    if buf is None:
        buf = jnp.zeros((batch, hist, width), F32)
    else:
        buf = jnp.pad(buf, ((0, 0), (1, 0), (0, 0)))
    u3 = u.reshape(u.shape[0] // tb, tb, u.shape[1])
    r0 = row0 // (nb * tb)
    mixed, nbuf = pl.pallas_call(
        functools.partial(_pool_kernel, n_past=n_past),
        out_shape=[jax.ShapeDtypeStruct((batch * seq, width), MXU_DTYPE),
                   jax.ShapeDtypeStruct((batch, hist, width), F32)],
        grid=(batch // nb, nt),
        in_specs=[pl.BlockSpec((nb, tb, width), lambda b, n: (r0 + b * nt + n, 0, col // width)),
                  pl.BlockSpec((nb, hist, width), lambda b, n: (b, 0, 0)),
                  pl.BlockSpec((None, groups, gw, gw), lambda b, n: (l, 0, 0, 0)),
                  pl.BlockSpec((1, width), lambda b, n: (0, 0))],
        out_specs=[pl.BlockSpec((nb * tb, width), lambda b, n: (b * nt + n, 0)),
                   pl.BlockSpec((nb, hist, width), lambda b, n: (b, 0, 0))],
        scratch_shapes=[pltpu.VMEM((nb, hist + tb, width), F32)],
        compiler_params=_params("parallel", "arbitrary"),
        name="pool",
    )(u3, buf, w_grp, scale.reshape(1, width))
    return mixed, nbuf[:, 1:, :]


def _attend_kernel(q_ref, k_ref, v_ref, o_ref, *, tq, hd):
    nb = k_ref.shape[0]
    for b in range(nb):
        rows = slice(b * tq, (b + 1) * tq)
        for h in range(q_ref.shape[1] // hd):
            cols = slice(h * hd, (h + 1) * hd)
            k, v = k_ref[b, :, cols], v_ref[b, :, cols]
            q = q_ref[rows, cols].astype(MXU_DTYPE)
            s = lax.dot_general(q, k.astype(MXU_DTYPE), (((1,), (1,)), ((), ())),
                                preferred_element_type=F32) * (hd ** -0.5)
            e = jnp.exp(s - jnp.max(s, axis=-1, keepdims=True))
            a = e / jnp.sum(e, axis=-1, keepdims=True)
            o_ref[rows, cols] = jnp.dot(a.astype(MXU_DTYPE), v.astype(MXU_DTYPE),
                                        preferred_element_type=F32).astype(o_ref.dtype)


def _attend(u, k_arr, v_arr, k_spec, v_spec, hd, hps, col, row0, batch, seq, *, nb, tq):
    nb, tq = _tile(batch, nb), _tile(seq, tq)
    nt = seq // tq
    wq = hps * hd
    assert col % wq == 0 and row0 % (nb * tq) == 0 and (nb == 1 or nt == 1) and MEM_HEADS % hps == 0
    r0, c0 = row0 // (nb * tq), col // wq
    return pl.pallas_call(
        functools.partial(_attend_kernel, tq=tq, hd=hd),
        out_shape=jax.ShapeDtypeStruct((batch * seq, MEM_HEADS * hd), MXU_DTYPE),
        grid=(batch // nb, nt, MEM_HEADS // hps),
        in_specs=[pl.BlockSpec((nb * tq, wq), lambda b, n, h: (r0 + b * nt + n, c0 + h)), k_spec(nb), v_spec(nb)],
        out_specs=pl.BlockSpec((nb * tq, wq), lambda b, n, h: (b * nt + n, h)),
        compiler_params=_params("parallel", "arbitrary", "arbitrary"),
        name="attend",
    )(u, k_arr, v_arr)


LANES = 128


def _cache_rows(cache):
    depth, batch, tokens, heads, hd = cache.shape
    lt = hd // LANES
    c = cache.reshape(depth, batch, tokens, heads, lt, LANES).transpose(0, 1, 2, 4, 3, 5)
    return c.reshape(depth, batch, tokens * lt * heads, LANES)


def _attend_rows_kernel(q_ref, k_ref, v_ref, o_ref, *, tq, hd):
    nb, n_rows, _ = k_ref.shape
    nlt = hd // LANES
    rper = MEM_HEADS * nlt
    nq = MEM_HEADS * tq
    row = lax.broadcasted_iota(jnp.int32, (nq, n_rows), 0)
    col = lax.broadcasted_iota(jnp.int32, (nq, n_rows), 1)
    valid = (col % rper) == (row // tq)
    for b in range(nb):
        rows = slice(b * tq, (b + 1) * tq)
        qst = jnp.concatenate([q_ref[rows, h * hd + lt * LANES:h * hd + (lt + 1) * LANES]
                               for lt in range(nlt) for h in range(MEM_HEADS)], axis=0).astype(MXU_DTYPE)
        p = lax.dot_general(qst, k_ref[b].astype(MXU_DTYPE), (((1,), (1,)), ((), ())), preferred_element_type=F32)
        s = p[0:nq]
        for lt in range(1, nlt):
            s = s + pltpu.roll(p[lt * nq:(lt + 1) * nq], n_rows - lt * MEM_HEADS, 1)
        s = jnp.where(valid, s * (hd ** -0.5), NEG_BIG)
        e = jnp.exp(s - jnp.max(s, axis=-1, keepdims=True))
        a = e / jnp.sum(e, axis=-1, keepdims=True)
        ast = jnp.concatenate([a if lt == 0 else pltpu.roll(a, lt * MEM_HEADS, 1) for lt in range(nlt)], axis=0)
        o = jnp.dot(ast.astype(MXU_DTYPE), v_ref[b].astype(MXU_DTYPE), preferred_element_type=F32)
        for lt in range(nlt):
            for h in range(MEM_HEADS):
                r0 = lt * nq + h * tq
                o_ref[rows, h * hd + lt * LANES:h * hd + (lt + 1) * LANES] = o[r0:r0 + tq].astype(o_ref.dtype)


def _attend_rows(u, k_rows, v_rows, l, hd, col, row0, batch, tq, *, nb):
    width = MEM_HEADS * hd
    nb = _tile(batch, nb)
    n_rows = k_rows.shape[2]
    assert col % width == 0 and row0 % (nb * tq) == 0 and hd % LANES == 0
    r0, c0 = row0 // (nb * tq), col // width
    kv_spec = pl.BlockSpec((None, nb, n_rows, LANES), lambda b: (l, b, 0, 0))
    return pl.pallas_call(
        functools.partial(_attend_rows_kernel, tq=tq, hd=hd),
        out_shape=jax.ShapeDtypeStruct((batch * tq, width), MXU_DTYPE),
        grid=(batch // nb,),
        in_specs=[pl.BlockSpec((nb * tq, width), lambda b: (r0 + b, c0)), kv_spec, kv_spec],
        out_specs=pl.BlockSpec((nb * tq, width), lambda b: (b, 0)),
        compiler_params=_params("parallel"),
        name="attend_rows",
    )(u, k_rows, v_rows)


PAST_LEN = 16384


def kernel(x_prompt, x_sample, state_hgrn, state_pool, cache_mem_k, cache_mem_v, mem_prompt, norm_pre_mix, norm_post_mix, norm_pre_mlp, norm_post_mlp, norm_mem, w_in, hgrn_lb, hgrn_out_norm, w_pool, pool_scale, w_mem_kv, w_branch_hgrn, w_branch_pool, w_branch_mem, w_out, w_up, w_down):
    bp, sp, d = x_prompt.shape
    bs, ss, _ = x_sample.shape
    depth = w_in.shape[0]
    hw, pw, mw = hgrn_lb.shape[1], pool_scale.shape[1], w_branch_mem.shape[1]
    mp, ms = bp * sp, bs * ss
    tokens = mem_prompt.shape[1]
    heads = hw // HGRN_HEAD_DIM
    pool_col, mem_col, gate_col = 4 * hw, 4 * hw + pw, 4 * hw + pw + mw

    lb_soft = jax.nn.softmax(hgrn_lb.astype(F32), axis=0)
    lb_all = jnp.cumsum(lb_soft, axis=0) - lb_soft[0:1]

    x = jnp.concatenate([x_prompt.reshape(mp, d), x_sample.reshape(ms, d)], axis=0)
    mem2 = mem_prompt.reshape(bp * tokens, d)
    h, h_scale = _rmsnorm(x, norm_pre_mix[0]), None
    hd = mw // MEM_HEADS
    if hd % LANES == 0:
        k_rows, v_rows = _cache_rows(cache_mem_k), _cache_rows(cache_mem_v)
    outs = [[] for _ in range(5)]
    st_s = None
    for l in range(depth):
        kv = _proj(_rmsnorm(mem2, norm_mem[l]), w_mem_kv, l, out_dtype=F32, tm=1024)
        kv3 = kv.reshape(bp, tokens, 2 * mw)
        u = _proj(h, w_in, l, h_scale, out_dtype=F32)

        o_p, st_p = _hgrn_prompt(u, lb_all[l], hgrn_out_norm[l], bp, sp, hw)
        o_s, st_s = _hgrn_sample(u, lb_all[l], hgrn_out_norm[l], state_hgrn, l, st_s, mp, bs, ss, hw)
        pool_p, buf_p = _pool(u, None, w_pool, l, pool_scale[l], pool_col, 0, bp, sp, 0, nb=1, tb=512)
        pool_s, buf_s = _pool(u, state_pool[l], w_pool, l, pool_scale[l], pool_col, mp, bs, ss, PAST_LEN,
                              nb=32, tb=ss)
        kv_spec = lambda c: lambda nb: pl.BlockSpec((nb, tokens, mw), lambda b, n, h: (b, 0, c))
        mem_p = _attend(u, kv3, kv3, kv_spec(0), kv_spec(1), hd, MEM_HEADS, mem_col, 0, bp, sp, nb=1, tq=512)
        if hd % LANES == 0:
            mem_s = _attend_rows(u, k_rows, v_rows, l, hd, mem_col, mp, bs, ss, nb=4)
        else:
            mem_s = _attend(u, cache_mem_k[l].reshape(bs, tokens, mw), cache_mem_v[l].reshape(bs, tokens, mw),
                            kv_spec(0), kv_spec(0), hd, MEM_HEADS, mem_col, mp, bs, ss, nb=4, tq=ss)

        merged = _merge((o_p, pool_p, mem_p), (o_s, pool_s, mem_s), w_branch_hgrn, w_branch_pool, w_branch_mem, l,
                        u, gate_col)
        x, h2, h2_scale = _resid(merged, w_out, l, x, norm_post_mix[l], norm_pre_mlp[l])
        ff = _proj(h2, w_up, l, h2_scale, out_dtype=MXU_DTYPE, relu2=True)
        x, h, h_scale = _resid(ff, w_down, l, x, norm_post_mlp[l], norm_pre_mix[l + 1] if l + 1 < depth else None)

        for lst, val in zip(outs, (st_p, buf_p, kv3[..., :mw].reshape(bp, tokens, MEM_HEADS, hd),
                                   kv3[..., mw:].reshape(bp, tokens, MEM_HEADS, hd), buf_s)):
            lst.append(val)

    stacked = [jnp.stack(o) for o in outs]
    return (x[:mp].reshape(bp, sp, d), x[mp:].reshape(bs, ss, d), *stacked[:4], st_s, stacked[4])
```

```python
import functools
import math

import jax
import jax.numpy as jnp
from jax import lax
from jax.experimental import pallas as pl
from jax.experimental.pallas import tpu as pltpu

F32 = jnp.float32
MXU_DTYPE = jnp.bfloat16
EPS = 1e-6
F_FLOOR = 1e-30
NEG_BIG = -1e30
HGRN_HEAD_DIM = 128
POOL_WINDOWS = (2, 4, 8, 16)
POOL_HIST = 16
MEM_HEADS = 4
N_BRANCHES = 3
V7X_VMEM_LIMIT_BYTES = 56 * 1024 * 1024
HGRN_CHUNK = 128
HGRN_DIAG = 8


def _params(*sem):
    return pltpu.CompilerParams(dimension_semantics=sem, vmem_limit_bytes=V7X_VMEM_LIMIT_BYTES)


def _tile(n, t, quantum=1):
    if n <= t:
        return n
    t -= t % quantum
    while n % t:
        t -= quantum
    assert t > 0, (n, quantum)
    return t


def _rms_scale(x):
    return lax.rsqrt(jnp.mean(x * x, axis=-1, keepdims=True) + EPS)


def _rmsnorm_kernel(x_ref, g_ref, o_ref):
    x = x_ref[...]
    o_ref[...] = (x * _rms_scale(x) * g_ref[...]).astype(o_ref.dtype)


def _rmsnorm(x, g, tm=256):
    m, d = x.shape
    tm = _tile(m, tm)
    return pl.pallas_call(
        _rmsnorm_kernel,
        out_shape=jax.ShapeDtypeStruct((m, d), MXU_DTYPE),
        grid=(m // tm,),
        in_specs=[pl.BlockSpec((tm, d), lambda i: (i, 0)), pl.BlockSpec((1, d), lambda i: (0, 0))],
        out_specs=pl.BlockSpec((tm, d), lambda i: (i, 0)),
        compiler_params=_params("parallel"),
        name="rmsnorm",
    )(x, g.reshape(1, d))


def _proj_kernel(a_ref, w_ref, *rest, relu2):
    o_ref = rest[-1]
    acc = jnp.dot(a_ref[...], w_ref[...].astype(MXU_DTYPE), preferred_element_type=F32)
    if len(rest) == 2:
        acc = acc * rest[0][...]
    if relu2:
        acc = jnp.square(jnp.maximum(acc, 0.0))
    o_ref[...] = acc.astype(o_ref.dtype)


def _proj(a, w, l, row_scale=None, *, out_dtype, relu2=False, tm=1536, tn=512):
    m, k = a.shape
    n = w.shape[2]
    tm, tn = _tile(m, tm), _tile(n, tn, LANES)
    in_specs = [pl.BlockSpec((tm, k), lambda i, j: (i, 0), pipeline_mode=pl.Buffered(1)),
                pl.BlockSpec((None, k, tn), lambda i, j: (l, 0, j))]
    args = [a, w]
    if row_scale is not None:
        in_specs.append(pl.BlockSpec((tm, 1), lambda i, j: (i, 0)))
        args.append(row_scale)
    return pl.pallas_call(
        functools.partial(_proj_kernel, relu2=relu2),
        out_shape=jax.ShapeDtypeStruct((m, n), out_dtype),
        grid=(m // tm, n // tn),
        in_specs=in_specs,
        out_specs=pl.BlockSpec((tm, tn), lambda i, j: (i, j)),
        compiler_params=_params("parallel", "arbitrary"),
        name="proj",
    )(*args)


def _merge_kernel(*refs, n_first):
    w_refs, g_refs, o_ref = refs[6:9], refs[9:12], refs[12]

    def merged(a_refs):
        tot = None
        for a_ref, w_ref, g_ref in zip(a_refs, w_refs, g_refs):
            y = jnp.dot(a_ref[...], w_ref[...].astype(MXU_DTYPE), preferred_element_type=F32)
            y = jax.nn.sigmoid(g_ref[...]) * y
            tot = y if tot is None else tot + y
        o_ref[...] = tot.astype(o_ref.dtype)

    first = pl.program_id(0) < n_first
    pl.when(first)(lambda: merged(refs[0:3]))
    pl.when(jnp.logical_not(first))(lambda: merged(refs[3:6]))


def _merge(branches_a, branches_b, w_h, w_p, w_m, l, u, gate_col, *, tm=1024, tn=512):
    ma, mb = branches_a[0].shape[0], branches_b[0].shape[0]
    d = w_h.shape[2]
    tm = _tile(mb, _tile(ma, tm))
    tn = math.gcd(_tile(d, tn, LANES), gate_col)
    gc, nd, na = gate_col // tn, d // tn, ma // tm
    once = pl.Buffered(1)
    a_spec = lambda a: pl.BlockSpec((tm, a.shape[1]), lambda i, j: (jnp.minimum(i, na - 1), 0), pipeline_mode=once)
    b_spec = lambda a: pl.BlockSpec((tm, a.shape[1]), lambda i, j: (jnp.maximum(i - na, 0), 0), pipeline_mode=once)
    w_spec = lambda w: pl.BlockSpec((None, w.shape[1], tn), lambda i, j: (l, 0, j))
    g_spec = lambda b: pl.BlockSpec((tm, tn), lambda i, j: (i, gc + b * nd + j))
    return pl.pallas_call(
        functools.partial(_merge_kernel, n_first=na),
        out_shape=jax.ShapeDtypeStruct((ma + mb, d), MXU_DTYPE),
        grid=((ma + mb) // tm, nd),
        in_specs=[a_spec(a) for a in branches_a] + [b_spec(a) for a in branches_b]
                 + [w_spec(w_h), w_spec(w_p), w_spec(w_m), g_spec(0), g_spec(1), g_spec(2)],
        out_specs=pl.BlockSpec((tm, tn), lambda i, j: (i, j)),
        compiler_params=_params("parallel", "arbitrary"),
        name="merge",
    )(*branches_a, *branches_b, w_h, w_p, w_m, u, u, u)


def _resid_kernel(a_ref, w_ref, x_ref, gpost_ref, gnext_ref, ox_ref, *rest, nt, nk, nc, tc, want_h):
    if want_h:
        oh_ref, os_ref, acc_ref, s1_ref, s2_ref = rest
    else:
        acc_ref, s1_ref, s2_ref = rest
    i, s = pl.program_id(0), pl.program_id(1)
    cur = i % 2
    prev = 1 - cur
    n_total = nc * tc
    has_prev = i > 0

    @pl.when(jnp.logical_and(has_prev, s == 0))
    def _():
        ss = jnp.zeros(s1_ref.shape, F32)
        for c in range(nc):
            y = acc_ref[prev, c]
            ss = ss + jnp.sum(y * y, axis=-1, keepdims=True)
        s1_ref[...] = lax.rsqrt(ss / n_total + EPS)
        s2_ref[...] = jnp.zeros(s2_ref.shape, F32)

    def epilogue_chunk():
        xn = x_ref[...] + acc_ref[prev, s] * s1_ref[...] * gpost_ref[...]
        ox_ref[...] = xn
        if want_h:
            oh_ref[...] = (xn * gnext_ref[...]).astype(oh_ref.dtype)
            s2_ref[...] += jnp.sum(xn * xn, axis=-1, keepdims=True)

    def k_step(first):
        a = a_ref[...]
        w = w_ref[...].astype(MXU_DTYPE)
        for c in range(nc):
            y = jnp.dot(a, w[:, c * tc:(c + 1) * tc], preferred_element_type=F32)
            if first:
                acc_ref[cur, c] = y
            else:
                acc_ref[cur, c] += y

    do_e = jnp.logical_and(has_prev, s < nc)
    do_k = i < nt
    for e_on in (True, False):
        for first in (True, False, None):
            cond = do_e if e_on else jnp.logical_not(do_e)
            if first is None:
                cond = jnp.logical_and(cond, jnp.logical_not(do_k))
            else:
                cond = jnp.logical_and(jnp.logical_and(cond, do_k), (s == 0) if first else (s > 0))
            if not e_on and first is None:
                continue

            @pl.when(cond)
            def _(e_on=e_on, first=first):
                if e_on:
                    epilogue_chunk()
                if first is not None:
                    k_step(first)

    if want_h:
        @pl.when(jnp.logical_and(has_prev, s == nc - 1))
        def _():
            os_ref[...] = lax.rsqrt(s2_ref[...] / n_total + EPS)


def _resid(a, w, l, x, g_post, g_next, *, tm=768, tk=512, tc=512):
    m, k = a.shape
    n = w.shape[2]
    tm, tk, tc = _tile(m, tm), _tile(k, tk, LANES), _tile(n, tc, LANES)
    nt, nk, nc = m // tm, k // tk, n // tc
    assert nk >= nc, (nk, nc)
    want_h = g_next is not None
    if not want_h:
        g_next = g_post
    ti = lambda i: jnp.minimum(i, nt - 1)
    kk = lambda i, s: jnp.where(i < nt, s, nk - 1)
    pi = lambda i: jnp.maximum(i - 1, 0)
    cc = lambda i, s: jnp.where(i > 0, jnp.minimum(s, nc - 1), 0)
    out_shape = [jax.ShapeDtypeStruct((m, n), F32)]
    out_specs = [pl.BlockSpec((tm, tc), lambda i, s: (pi(i), cc(i, s)))]
    if want_h:
        out_shape += [jax.ShapeDtypeStruct((m, n), MXU_DTYPE), jax.ShapeDtypeStruct((m, 1), F32)]
        out_specs += [pl.BlockSpec((tm, tc), lambda i, s: (pi(i), cc(i, s))),
                      pl.BlockSpec((tm, 1), lambda i, s: (pi(i), 0))]
    outs = pl.pallas_call(
        functools.partial(_resid_kernel, nt=nt, nk=nk, nc=nc, tc=tc, want_h=want_h),
        out_shape=out_shape,
        grid=(nt + 1, nk),
        in_specs=[pl.BlockSpec((tm, tk), lambda i, s: (ti(i), kk(i, s))),
                  pl.BlockSpec((None, tk, n), lambda i, s: (l, kk(i, s), 0)),
                  pl.BlockSpec((tm, tc), lambda i, s: (pi(i), cc(i, s))),
                  pl.BlockSpec((1, tc), lambda i, s: (0, cc(i, s))),
                  pl.BlockSpec((1, tc), lambda i, s: (0, cc(i, s)))],
        out_specs=out_specs,
        scratch_shapes=[pltpu.VMEM((2, nc, tm, tc), F32), pltpu.VMEM((tm, 1), F32), pltpu.VMEM((tm, 1), F32)],
        compiler_params=_params("arbitrary", "arbitrary"),
        name="resid",
    )(a, w, x, g_post.reshape(1, n), g_next.reshape(1, n))
    return tuple(outs) if want_h else (outs[0], None, None)


def _hgrn_gates(uq, uf, lb):
    f = lb + (1.0 - lb) * jax.nn.sigmoid(uf)
    logf = jnp.log(jnp.maximum(f, F_FLOOR))
    q = jax.nn.silu(uq) * (HGRN_HEAD_DIM ** -0.5)
    return q, 1.0 - f, logf


def _hgrn_finish(o, ug, gain, dtype):
    o = o * _rms_scale(o) * gain
    return (o * jax.nn.silu(ug)).astype(dtype)


def _split3(x):
    h1 = x.astype(MXU_DTYPE)
    r = x - h1.astype(F32)
    h2 = r.astype(MXU_DTYPE)
    h3 = (r - h2.astype(F32)).astype(MXU_DTYPE)
    return h1, h2, h3


def _diag_offsets(q, k, cum, row, block):
    yield 0, jnp.sum(q * k, axis=-1, keepdims=True)
    for j in range(1, block):
        inside = (row % block) >= j
        d = jnp.where(inside, cum - pltpu.roll(cum, j, 0), NEG_BIG)
        yield j, jnp.sum(q * pltpu.roll(k, j, 0) * jnp.exp(d), axis=-1, keepdims=True)


def _diag_blocks(q, k, v, cum, tmp_ref):
    g = HGRN_DIAG
    n = q.shape[0] // g
    for i, a in enumerate((q, k, v, cum)):
        tmp_ref[i] = a
    views = [[tmp_ref[i, pl.ds(r, n, stride=g), :] for r in range(g)] for i in range(4)]
    qs, ks, vs, cs = views
    for r in range(g):
        acc = jnp.sum(qs[r] * ks[r], axis=-1, keepdims=True) * vs[r]
        for r2 in range(r):
            p = qs[r] * ks[r2] * jnp.exp(cs[r] - cs[r2])
            acc = acc + jnp.sum(p, axis=-1, keepdims=True) * vs[r2]
        tmp_ref[4, pl.ds(r, n, stride=g), :] = acc
    return tmp_ref[4]


def _hgrn_chunk(q, k, v, logf, st, tmp_ref):
    c = q.shape[0]
    row = lax.broadcasted_iota(jnp.int32, (c, 1), 0)
    rr = lax.broadcasted_iota(jnp.int32, (c, c), 0)
    cc = lax.broadcasted_iota(jnp.int32, (c, c), 1)
    tri = (cc <= rr).astype(MXU_DTYPE)
    cum = sum(jnp.dot(tri, h, preferred_element_type=F32) for h in _split3(logf)[::-1])

    o_diag = _diag_blocks(q, k, v, cum, tmp_ref)
    scores = jnp.zeros((c, c), F32)
    b = HGRN_DIAG
    while b < c:
        c3 = cum.reshape(c // (2 * b), 2 * b, cum.shape[1])
        e = jnp.exp(-jnp.abs(c3 - c3[:, b - 1:b, :])).reshape(cum.shape)
        odd = ((row // b) % 2) == 1
        qt = jnp.where(odd, q * e, 0.0).astype(MXU_DTYPE)
        kt = jnp.where(odd, 0.0, k * e).astype(MXU_DTYPE)
        s = lax.dot_general(qt, kt, (((1,), (1,)), ((), ())), preferred_element_type=F32)
        scores = scores + jnp.where((rr // (2 * b)) == (cc // (2 * b)), s, 0.0)
        b *= 2

    vm = v.astype(MXU_DTYPE)
    qd = (q * jnp.exp(cum)).astype(MXU_DTYPE)
    o = lax.dot_general(qd, st.astype(MXU_DTYPE), (((1,), (1,)), ((), ())), preferred_element_type=F32)
    o = o + jnp.dot(scores.astype(MXU_DTYPE), vm, preferred_element_type=F32) + o_diag
    last = cum[c - 1:c, :]
    kd = (k * jnp.exp(last - cum)).astype(MXU_DTYPE)
    st_new = st * jnp.exp(last) + jnp.dot(v.T.astype(MXU_DTYPE), kd, preferred_element_type=F32)
    return o, st_new


def _hgrn_prompt_kernel(uq_ref, uf_ref, ui_ref, ug_ref, lb_ref, gain_ref, o_ref, s_ref, st_ref, tmp_ref, *, chunk):
    n = pl.program_id(2)

    @pl.when(n == 0)
    def _():
        st_ref[...] = jnp.zeros(st_ref.shape, F32)

    dk = HGRN_HEAD_DIM
    for ci in range(uq_ref.shape[0] // chunk):
        rows = pl.ds(ci * chunk, chunk)
        for hh in range(uq_ref.shape[1] // dk):
            cols = slice(hh * dk, (hh + 1) * dk)
            q, k, logf = _hgrn_gates(uq_ref[rows, cols], uf_ref[rows, cols], lb_ref[:, cols])
            o, st_new = _hgrn_chunk(q, k, ui_ref[rows, cols], logf, st_ref[hh], tmp_ref.at[hh])
            st_ref[hh] = st_new
            o_ref[rows, cols] = _hgrn_finish(o, ug_ref[rows, cols], gain_ref[:, cols], o_ref.dtype)

    @pl.when(n == pl.num_programs(2) - 1)
    def _():
        for hh in range(st_ref.shape[0]):
            s_ref[hh] = st_ref[hh].T


def _hgrn_prompt(u, lb, gain, batch, seq, width, *, tb=512, hp=4):
    dk = HGRN_HEAD_DIM
    heads = width // dk
    chunk = min(HGRN_CHUNK, seq)
    tb, hp = _tile(seq, tb), _tile(heads, hp)
    assert tb % chunk == 0
    nt, hg = seq // tb, heads // hp
    u_spec = lambda c: pl.BlockSpec((tb, hp * dk), lambda b, h, n: (b * nt + n, c * hg + h))
    vec_spec = pl.BlockSpec((1, hp * dk), lambda b, h, n: (0, h))
    return pl.pallas_call(
        functools.partial(_hgrn_prompt_kernel, chunk=chunk),
        out_shape=[jax.ShapeDtypeStruct((batch * seq, width), MXU_DTYPE),
                   jax.ShapeDtypeStruct((batch, heads, dk, dk), F32)],
        grid=(batch, hg, nt),
        in_specs=[u_spec(0), u_spec(1), u_spec(2), u_spec(3), vec_spec, vec_spec],
        out_specs=[pl.BlockSpec((tb, hp * dk), lambda b, h, n: (b * nt + n, h)),
                   pl.BlockSpec((None, hp, dk, dk), lambda b, h, n: (b, h, 0, 0))],
        scratch_shapes=[pltpu.VMEM((hp, dk, dk), F32), pltpu.VMEM((hp, 5, chunk, dk), F32)],
        compiler_params=_params("parallel", "parallel", "arbitrary"),
        name="hgrn_prompt",
    )(u, u, u, u, lb.reshape(1, width), gain.reshape(1, width))


def _hgrn_sample_kernel(uq_ref, uf_ref, ui_ref, ug_ref, lb_ref, gain_ref, s0_ref, *rest, t):
    o_ref, s_ref = rest[-2:]
    n_prev = s_ref.shape[0] - 1
    for i in range(n_prev):
        s_ref[i] = rest[0][i]
    nb = s0_ref.shape[0]
    width = uq_ref.shape[1]
    dk = HGRN_HEAD_DIM
    heads = width // dk
    pad = dk - 2 * t
    assert pad >= 0
    lb = lb_ref[...]
    row = lax.broadcasted_iota(jnp.int32, (t, 1), 0)
    outs = [[] for _ in range(heads)]
    for sq in range(nb):
        rows = pl.ds(sq * t, t)
        q, k, logf = _hgrn_gates(uq_ref[rows, :], uf_ref[rows, :], lb)
        v = ui_ref[rows, :]
        cum = logf
        sh = 1
        while sh < t:
            cum = cum + jnp.where(row >= sh, pltpu.roll(cum, sh, 0), 0.0)
            sh *= 2
        last = cum[t - 1:t, :]
        qd = q * jnp.exp(cum)
        kd = k * jnp.exp(last - cum)
        dec = jnp.exp(last)
        for h in range(heads):
            cols = slice(h * dk, (h + 1) * dk)
            vh = v[:, cols]
            s0 = s0_ref[sq, h]
            o = jnp.dot(qd[:, cols].astype(MXU_DTYPE), s0.astype(MXU_DTYPE), preferred_element_type=F32)
            for j, rs in _diag_offsets(q[:, cols], k[:, cols], cum[:, cols], row, t):
                o = o + rs * (vh if j == 0 else pltpu.roll(vh, j, 0))
            outs[h].append(o)
            stack_t = jnp.concatenate([kd[:, cols], jnp.broadcast_to(dec[:, cols], (t, dk)),
                                       jnp.zeros((pad, dk), F32)], axis=0).T
            v_pad = jnp.concatenate([vh, jnp.zeros((dk - t, dk), F32)], axis=0)
            s_ref[n_prev, sq, h] = stack_t[:, t:t + 1] * s0 + jnp.dot(stack_t.astype(MXU_DTYPE), v_pad.astype(MXU_DTYPE),
                                                               preferred_element_type=F32)
    for h in range(heads):
        cols = slice(h * dk, (h + 1) * dk)
        o = jnp.concatenate(outs[h], axis=0)
        o_ref[:, cols] = _hgrn_finish(o, ug_ref[:, cols], gain_ref[:, cols], o_ref.dtype)


def _hgrn_sample(u, lb, gain, s0, l, prev, row0, batch, t, width, *, nb=2):
    dk = HGRN_HEAD_DIM
    heads = width // dk
    nb = _tile(batch, nb)
    assert row0 % (nb * t) == 0 and (prev is None) == (l == 0)
    r0 = row0 // (nb * t)
    u_spec = lambda c: pl.BlockSpec((nb * t, width), lambda b: (r0 + b, c))
    vec_spec = pl.BlockSpec((1, width), lambda b: (0, 0))
    stack_spec = lambda n: pl.BlockSpec((n, nb, heads, dk, dk), lambda b: (0, b, 0, 0, 0))
    in_specs = [u_spec(0), u_spec(1), u_spec(2), u_spec(3), vec_spec, vec_spec,
                pl.BlockSpec((None, nb, heads, dk, dk), lambda b: (l, b, 0, 0, 0))]
    args = [u, u, u, u, lb.reshape(1, width), gain.reshape(1, width), s0]
    if prev is not None:
        in_specs.append(stack_spec(l))
        args.append(prev)
    return pl.pallas_call(
        functools.partial(_hgrn_sample_kernel, t=t),
        out_shape=[jax.ShapeDtypeStruct((batch * t, width), MXU_DTYPE),
                   jax.ShapeDtypeStruct((l + 1, batch, heads, dk, dk), F32)],
        grid=(batch // nb,),
        in_specs=in_specs,
        out_specs=[pl.BlockSpec((nb * t, width), lambda b: (b, 0)), stack_spec(l + 1)],
        compiler_params=_params("parallel"),
        name="hgrn_sample",
    )(*args)


def _pool_kernel(p_ref, buf_ref, w_ref, scale_ref, o_ref, nbuf_ref, ext_ref, *, n_past):
    n = pl.program_id(1)
    nb, tb, width = p_ref.shape
    gw = width // len(POOL_WINDOWS)
    hist = POOL_HIST

    @pl.when(n == 0)
    def _():
        ext_ref[:, 0:hist, :] = buf_ref[...]

    @pl.when(n > 0)
    def _():
        ext_ref[:, 0:hist, :] = ext_ref[:, tb:tb + hist, :]

    ext_ref[:, hist:hist + tb, :] = p_ref[...]
    pos = n_past + n * tb + lax.broadcasted_iota(jnp.int32, (1, tb, 1), 1)
    for g, win in enumerate(POOL_WINDOWS):
        cols = slice(g * gw, (g + 1) * gw)
        tot = ext_ref[:, hist:hist + tb, cols]
        x = tot
        for i in range(1, win):
            tot = tot + ext_ref[:, hist - i:hist - i + tb, cols]
        count = jnp.minimum(pos + 1, win).astype(F32)
        pooled = (tot / count - x).reshape(nb * tb, gw)
        mixed = jnp.dot(pooled.astype(MXU_DTYPE), w_ref[g].astype(MXU_DTYPE), preferred_element_type=F32)
        o_ref[:, cols] = (mixed * scale_ref[:, cols]).astype(o_ref.dtype)

    @pl.when(n == pl.num_programs(1) - 1)
    def _():
        nbuf_ref[...] = ext_ref[:, tb:tb + hist, :]


def _pool(u, buf, w_grp, l, scale, col, row0, batch, seq, n_past, *, nb, tb):
    _, groups, gw, _ = w_grp.shape
    width = groups * gw
    hist = POOL_HIST
    nb, tb = _tile(batch, nb), _tile(seq, tb)
    nt = seq // tb
    assert col % width == 0 and row0 % (nb * tb) == 0 and (nb == 1 or nt == 1) and tb % 8 == 0
    if buf is None:
        buf = jnp.zeros((batch, hist, width), F32)
    else:
        buf = jnp.pad(buf, ((0, 0), (1, 0), (0, 0)))
    u3 = u.reshape(u.shape[0] // tb, tb, u.shape[1])
    r0 = row0 // (nb * tb)
    mixed, nbuf = pl.pallas_call(
        functools.partial(_pool_kernel, n_past=n_past),
        out_shape=[jax.ShapeDtypeStruct((batch * seq, width), MXU_DTYPE),
                   jax.ShapeDtypeStruct((batch, hist, width), F32)],
        grid=(batch // nb, nt),
        in_specs=[pl.BlockSpec((nb, tb, width), lambda b, n: (r0 + b * nt + n, 0, col // width)),
                  pl.BlockSpec((nb, hist, width), lambda b, n: (b, 0, 0)),
                  pl.BlockSpec((None, groups, gw, gw), lambda b, n: (l, 0, 0, 0)),
                  pl.BlockSpec((1, width), lambda b, n: (0, 0))],
        out_specs=[pl.BlockSpec((nb * tb, width), lambda b, n: (b * nt + n, 0)),
                   pl.BlockSpec((nb, hist, width), lambda b, n: (b, 0, 0))],
        scratch_shapes=[pltpu.VMEM((nb, hist + tb, width), F32)],
        compiler_params=_params("parallel", "arbitrary"),
        name="pool",
    )(u3, buf, w_grp, scale.reshape(1, width))
    return mixed, nbuf[:, 1:, :]


def _attend_kernel(q_ref, k_ref, v_ref, o_ref, *, tq, hd):
    nb = k_ref.shape[0]
    for b in range(nb):
        rows = slice(b * tq, (b + 1) * tq)
        for h in range(q_ref.shape[1] // hd):
            cols = slice(h * hd, (h + 1) * hd)
            k, v = k_ref[b, :, cols], v_ref[b, :, cols]
            q = q_ref[rows, cols].astype(MXU_DTYPE)
            s = lax.dot_general(q, k.astype(MXU_DTYPE), (((1,), (1,)), ((), ())),
                                preferred_element_type=F32) * (hd ** -0.5)
            e = jnp.exp(s - jnp.max(s, axis=-1, keepdims=True))
            a = e / jnp.sum(e, axis=-1, keepdims=True)
            o_ref[rows, cols] = jnp.dot(a.astype(MXU_DTYPE), v.astype(MXU_DTYPE),
                                        preferred_element_type=F32).astype(o_ref.dtype)


def _attend(u, k_arr, v_arr, k_spec, v_spec, hd, hps, col, row0, batch, seq, *, nb, tq):
    nb, tq = _tile(batch, nb), _tile(seq, tq)
    nt = seq // tq
    wq = hps * hd
    assert col % wq == 0 and row0 % (nb * tq) == 0 and (nb == 1 or nt == 1) and MEM_HEADS % hps == 0
    r0, c0 = row0 // (nb * tq), col // wq
    return pl.pallas_call(
        functools.partial(_attend_kernel, tq=tq, hd=hd),
        out_shape=jax.ShapeDtypeStruct((batch * seq, MEM_HEADS * hd), MXU_DTYPE),
        grid=(batch // nb, nt, MEM_HEADS // hps),
        in_specs=[pl.BlockSpec((nb * tq, wq), lambda b, n, h: (r0 + b * nt + n, c0 + h)), k_spec(nb), v_spec(nb)],
        out_specs=pl.BlockSpec((nb * tq, wq), lambda b, n, h: (b * nt + n, h)),
        compiler_params=_params("parallel", "arbitrary", "arbitrary"),
        name="attend",
    )(u, k_arr, v_arr)


LANES = 128


def _cache_rows(cache):
    depth, batch, tokens, heads, hd = cache.shape
    lt = hd // LANES
    c = cache.reshape(depth, batch, tokens, heads, lt, LANES).transpose(0, 1, 2, 4, 3, 5)
    return c.reshape(depth, batch, tokens * lt * heads, LANES)


def _attend_rows_kernel(q_ref, k_ref, v_ref, o_ref, *, tq, hd):
    nb, n_rows, _ = k_ref.shape
    nlt = hd // LANES
    rper = MEM_HEADS * nlt
    nq = MEM_HEADS * tq
    row = lax.broadcasted_iota(jnp.int32, (nq, n_rows), 0)
    col = lax.broadcasted_iota(jnp.int32, (nq, n_rows), 1)
    valid = (col % rper) == (row // tq)
    for b in range(nb):
        rows = slice(b * tq, (b + 1) * tq)
        qst = jnp.concatenate([q_ref[rows, h * hd + lt * LANES:h * hd + (lt + 1) * LANES]
                               for lt in range(nlt) for h in range(MEM_HEADS)], axis=0).astype(MXU_DTYPE)
        p = lax.dot_general(qst, k_ref[b].astype(MXU_DTYPE), (((1,), (1,)), ((), ())), preferred_element_type=F32)
        s = p[0:nq]
        for lt in range(1, nlt):
            s = s + pltpu.roll(p[lt * nq:(lt + 1) * nq], n_rows - lt * MEM_HEADS, 1)
        s = jnp.where(valid, s * (hd ** -0.5), NEG_BIG)
        e = jnp.exp(s - jnp.max(s, axis=-1, keepdims=True))
        a = e / jnp.sum(e, axis=-1, keepdims=True)
        ast = jnp.concatenate([a if lt == 0 else pltpu.roll(a, lt * MEM_HEADS, 1) for lt in range(nlt)], axis=0)
        o = jnp.dot(ast.astype(MXU_DTYPE), v_ref[b].astype(MXU_DTYPE), preferred_element_type=F32)
        for lt in range(nlt):
            for h in range(MEM_HEADS):
                r0 = lt * nq + h * tq
                o_ref[rows, h * hd + lt * LANES:h * hd + (lt + 1) * LANES] = o[r0:r0 + tq].astype(o_ref.dtype)


def _attend_rows(u, k_rows, v_rows, l, hd, col, row0, batch, tq, *, nb):
    width = MEM_HEADS * hd
    nb = _tile(batch, nb)
    n_rows = k_rows.shape[2]
    assert col % width == 0 and row0 % (nb * tq) == 0 and hd % LANES == 0
    r0, c0 = row0 // (nb * tq), col // width
    kv_spec = pl.BlockSpec((None, nb, n_rows, LANES), lambda b: (l, b, 0, 0))
    return pl.pallas_call(
        functools.partial(_attend_rows_kernel, tq=tq, hd=hd),
        out_shape=jax.ShapeDtypeStruct((batch * tq, width), MXU_DTYPE),
        grid=(batch // nb,),
        in_specs=[pl.BlockSpec((nb * tq, width), lambda b: (r0 + b, c0)), kv_spec, kv_spec],
        out_specs=pl.BlockSpec((nb * tq, width), lambda b: (b, 0)),
        compiler_params=_params("parallel"),
        name="attend_rows",
    )(u, k_rows, v_rows)


PAST_LEN = 16384


def kernel(x_prompt, x_sample, state_hgrn, state_pool, cache_mem_k, cache_mem_v, mem_prompt, norm_pre_mix, norm_post_mix, norm_pre_mlp, norm_post_mlp, norm_mem, w_in, hgrn_lb, hgrn_out_norm, w_pool, pool_scale, w_mem_kv, w_branch_hgrn, w_branch_pool, w_branch_mem, w_out, w_up, w_down):
    bp, sp, d = x_prompt.shape
    bs, ss, _ = x_sample.shape
    depth = w_in.shape[0]
    hw, pw, mw = hgrn_lb.shape[1], pool_scale.shape[1], w_branch_mem.shape[1]
    mp, ms = bp * sp, bs * ss
    tokens = mem_prompt.shape[1]
    heads = hw // HGRN_HEAD_DIM
    pool_col, mem_col, gate_col = 4 * hw, 4 * hw + pw, 4 * hw + pw + mw

    lb_soft = jax.nn.softmax(hgrn_lb.astype(F32), axis=0)
    lb_all = jnp.cumsum(lb_soft, axis=0) - lb_soft[0:1]

    x = jnp.concatenate([x_prompt.reshape(mp, d), x_sample.reshape(ms, d)], axis=0)
    mem2 = mem_prompt.reshape(bp * tokens, d)
    h, h_scale = _rmsnorm(x, norm_pre_mix[0]), None
    hd = mw // MEM_HEADS
    if hd % LANES == 0:
        k_rows, v_rows = _cache_rows(cache_mem_k), _cache_rows(cache_mem_v)
    outs = [[] for _ in range(5)]
    st_s = None
    for l in range(depth):
        kv = _proj(_rmsnorm(mem2, norm_mem[l]), w_mem_kv, l, out_dtype=F32, tm=1024)
        kv3 = kv.reshape(bp, tokens, 2 * mw)
        u = _proj(h, w_in, l, h_scale, out_dtype=F32)

        o_p, st_p = _hgrn_prompt(u, lb_all[l], hgrn_out_norm[l], bp, sp, hw)
        o_s, st_s = _hgrn_sample(u, lb_all[l], hgrn_out_norm[l], state_hgrn, l, st_s, mp, bs, ss, hw)
        pool_p, buf_p = _pool(u, None, w_pool, l, pool_scale[l], pool_col, 0, bp, sp, 0, nb=1, tb=512)
        pool_s, buf_s = _pool(u, state_pool[l], w_pool, l, pool_scale[l], pool_col, mp, bs, ss, PAST_LEN,
                              nb=32, tb=ss)
        kv_spec = lambda c: lambda nb: pl.BlockSpec((nb, tokens, mw), lambda b, n, h: (b, 0, c))
        mem_p = _attend(u, kv3, kv3, kv_spec(0), kv_spec(1), hd, MEM_HEADS, mem_col, 0, bp, sp, nb=1, tq=512)
        if hd % LANES == 0:
            mem_s = _attend_rows(u, k_rows, v_rows, l, hd, mem_col, mp, bs, ss, nb=4)
        else:
            mem_s = _attend(u, cache_mem_k[l].reshape(bs, tokens, mw), cache_mem_v[l].reshape(bs, tokens, mw),
                            kv_spec(0), kv_spec(0), hd, MEM_HEADS, mem_col, mp, bs, ss, nb=4, tq=ss)

        merged = _merge((o_p, pool_p, mem_p), (o_s, pool_s, mem_s), w_branch_hgrn, w_branch_pool, w_branch_mem, l,
                        u, gate_col)
        x, h2, h2_scale = _resid(merged, w_out, l, x, norm_post_mix[l], norm_pre_mlp[l])
        ff = _proj(h2, w_up, l, h2_scale, out_dtype=MXU_DTYPE, relu2=True)
        x, h, h_scale = _resid(ff, w_down, l, x, norm_post_mlp[l], norm_pre_mix[l + 1] if l + 1 < depth else None)

        for lst, val in zip(outs, (st_p, buf_p, kv3[..., :mw].reshape(bp, tokens, MEM_HEADS, hd),
                                   kv3[..., mw:].reshape(bp, tokens, MEM_HEADS, hd), buf_s)):
            lst.append(val)

    stacked = [jnp.stack(o) for o in outs]
    return (x[:mp].reshape(bp, sp, d), x[mp:].reshape(bs, ss, d), *stacked[:4], st_s, stacked[4])
```

```python
import functools
import math

import jax
import jax.numpy as jnp
from jax import lax
from jax.experimental import pallas as pl
from jax.experimental.pallas import tpu as pltpu

F32 = jnp.float32
MXU_DTYPE = jnp.bfloat16
EPS = 1e-6
F_FLOOR = 1e-30
NEG_BIG = -1e30
HGRN_HEAD_DIM = 128
POOL_WINDOWS = (2, 4, 8, 16)
POOL_HIST = 16
MEM_HEADS = 4
N_BRANCHES = 3
V7X_VMEM_LIMIT_BYTES = 56 * 1024 * 1024
HGRN_CHUNK = 128
HGRN_DIAG = 8


def _params(*sem):
    return pltpu.CompilerParams(dimension_semantics=sem, vmem_limit_bytes=V7X_VMEM_LIMIT_BYTES)


def _tile(n, t, quantum=1):
    if n <= t:
        return n
    t -= t % quantum
    while n % t:
        t -= quantum
    assert t > 0, (n, quantum)
    return t


def _rms_scale(x):
    return lax.rsqrt(jnp.mean(x * x, axis=-1, keepdims=True) + EPS)


def _rmsnorm_kernel(x_ref, g_ref, o_ref):
    x = x_ref[...]
    o_ref[...] = (x * _rms_scale(x) * g_ref[...]).astype(o_ref.dtype)


def _rmsnorm(x, g, tm=256):
    m, d = x.shape
    tm = _tile(m, tm)
    return pl.pallas_call(
        _rmsnorm_kernel,
        out_shape=jax.ShapeDtypeStruct((m, d), MXU_DTYPE),
        grid=(m // tm,),
        in_specs=[pl.BlockSpec((tm, d), lambda i: (i, 0)), pl.BlockSpec((1, d), lambda i: (0, 0))],
        out_specs=pl.BlockSpec((tm, d), lambda i: (i, 0)),
        compiler_params=_params("parallel"),
        name="rmsnorm",
    )(x, g.reshape(1, d))


def _proj_kernel(*refs, relu2, has_scale, n_cast):
    a_ref, w_ref = refs[:2]
    acc = jnp.dot(a_ref[...], w_ref[...].astype(MXU_DTYPE), preferred_element_type=F32)
    if has_scale:
        acc = acc * refs[2][...]
    if relu2:
        acc = jnp.square(jnp.maximum(acc, 0.0))
    if n_cast:
        src_ref, o_ref, dst_ref = refs[-3:]

        @pl.when(pl.program_id(0) * pl.num_programs(1) + pl.program_id(1) < n_cast)
        def _():
            dst_ref[...] = src_ref[...].astype(dst_ref.dtype)
    else:
        o_ref = refs[-1]
    o_ref[...] = acc.astype(o_ref.dtype)


def _proj(a, w, l, row_scale=None, *, out_dtype, relu2=False, cast_src=None, tm=1536, tn=512, cast_rows=128):
    m, k = a.shape
    n = w.shape[2]
    tm, tn = _tile(m, tm), _tile(n, tn, LANES)
    nj = n // tn
    in_specs = [pl.BlockSpec((tm, k), lambda i, j: (i, 0), pipeline_mode=pl.Buffered(1)),
                pl.BlockSpec((None, k, tn), lambda i, j: (l, 0, j))]
    args = [a, w]
    if row_scale is not None:
        in_specs.append(pl.BlockSpec((tm, 1), lambda i, j: (i, 0)))
        args.append(row_scale)
    out_shape = [jax.ShapeDtypeStruct((m, n), out_dtype)]
    out_specs = [pl.BlockSpec((tm, tn), lambda i, j: (i, j))]
    n_cast = 0
    if cast_src is not None:
        _, r, c = cast_src.shape
        cast_rows = max(cast_rows, -(-r // ((m // tm) * nj)))
        while r % cast_rows:
            cast_rows += 1
        n_cast = r // cast_rows
        blk = lambda i, j: jnp.minimum(i * nj + j, n_cast - 1)
        in_specs.append(pl.BlockSpec((None, cast_rows, c), lambda i, j: (l, blk(i, j), 0)))
        args.append(cast_src)
        out_shape.append(jax.ShapeDtypeStruct((r, c), MXU_DTYPE))
        out_specs.append(pl.BlockSpec((cast_rows, c), lambda i, j: (blk(i, j), 0)))
    outs = pl.pallas_call(
        functools.partial(_proj_kernel, relu2=relu2, has_scale=row_scale is not None, n_cast=n_cast),
        out_shape=out_shape,
        grid=(m // tm, nj),
        in_specs=in_specs,
        out_specs=out_specs,
        compiler_params=_params("arbitrary", "arbitrary"),
        name="proj",
    )(*args)
    return (outs[0], outs[1]) if n_cast else outs[0]


def _merge_kernel(*refs, n_first):
    w_refs, g_refs, o_ref = refs[6:9], refs[9:12], refs[12]

    def merged(a_refs):
        tot = None
        for a_ref, w_ref, g_ref in zip(a_refs, w_refs, g_refs):
            y = jnp.dot(a_ref[...], w_ref[...].astype(MXU_DTYPE), preferred_element_type=F32)
            y = jax.nn.sigmoid(g_ref[...]) * y
            tot = y if tot is None else tot + y
        o_ref[...] = tot.astype(o_ref.dtype)

    first = pl.program_id(0) < n_first
    pl.when(first)(lambda: merged(refs[0:3]))
    pl.when(jnp.logical_not(first))(lambda: merged(refs[3:6]))


def _merge(branches_a, branches_b, w_h, w_p, w_m, l, u, gate_col, *, tm=1024, tn=512):
    ma, mb = branches_a[0].shape[0], branches_b[0].shape[0]
    d = w_h.shape[2]
    tm = _tile(mb, _tile(ma, tm))
    tn = math.gcd(_tile(d, tn, LANES), gate_col)
    gc, nd, na = gate_col // tn, d // tn, ma // tm
    once = pl.Buffered(1)
    a_spec = lambda a: pl.BlockSpec((tm, a.shape[1]), lambda i, j: (jnp.minimum(i, na - 1), 0), pipeline_mode=once)
    b_spec = lambda a: pl.BlockSpec((tm, a.shape[1]), lambda i, j: (jnp.maximum(i - na, 0), 0), pipeline_mode=once)
    w_spec = lambda w: pl.BlockSpec((None, w.shape[1], tn), lambda i, j: (l, 0, j))
    g_spec = lambda b: pl.BlockSpec((tm, tn), lambda i, j: (i, gc + b * nd + j))
    return pl.pallas_call(
        functools.partial(_merge_kernel, n_first=na),
        out_shape=jax.ShapeDtypeStruct((ma + mb, d), MXU_DTYPE),
        grid=((ma + mb) // tm, nd),
        in_specs=[a_spec(a) for a in branches_a] + [b_spec(a) for a in branches_b]
                 + [w_spec(w_h), w_spec(w_p), w_spec(w_m), g_spec(0), g_spec(1), g_spec(2)],
        out_specs=pl.BlockSpec((tm, tn), lambda i, j: (i, j)),
        compiler_params=_params("parallel", "arbitrary"),
        name="merge",
    )(*branches_a, *branches_b, w_h, w_p, w_m, u, u, u)


def _resid_kernel(a_ref, w_ref, x_ref, gpost_ref, gnext_ref, ox_ref, *rest, nt, nk, nc, tc, want_h):
    if want_h:
        oh_ref, os_ref, acc_ref, s1_ref, s2_ref = rest
    else:
        acc_ref, s1_ref, s2_ref = rest
    i, s = pl.program_id(0), pl.program_id(1)
    cur = i % 2
    prev = 1 - cur
    n_total = nc * tc
    has_prev = i > 0

    @pl.when(jnp.logical_and(has_prev, s == 0))
    def _():
        ss = jnp.zeros(s1_ref.shape, F32)
        for c in range(nc):
            y = acc_ref[prev, c]
            ss = ss + jnp.sum(y * y, axis=-1, keepdims=True)
        s1_ref[...] = lax.rsqrt(ss / n_total + EPS)
        s2_ref[...] = jnp.zeros(s2_ref.shape, F32)

    def epilogue_chunk():
        xn = x_ref[...] + acc_ref[prev, s] * s1_ref[...] * gpost_ref[...]
        ox_ref[...] = xn
        if want_h:
            oh_ref[...] = (xn * gnext_ref[...]).astype(oh_ref.dtype)
            s2_ref[...] += jnp.sum(xn * xn, axis=-1, keepdims=True)

    def k_step(first):
        a = a_ref[...]
        w = w_ref[...].astype(MXU_DTYPE)
        for c in range(nc):
            y = jnp.dot(a, w[:, c * tc:(c + 1) * tc], preferred_element_type=F32)
            if first:
                acc_ref[cur, c] = y
            else:
                acc_ref[cur, c] += y

    do_e = jnp.logical_and(has_prev, s < nc)
    do_k = i < nt
    for e_on in (True, False):
        for first in (True, False, None):
            cond = do_e if e_on else jnp.logical_not(do_e)
            if first is None:
                cond = jnp.logical_and(cond, jnp.logical_not(do_k))
            else:
                cond = jnp.logical_and(jnp.logical_and(cond, do_k), (s == 0) if first else (s > 0))
            if not e_on and first is None:
                continue

            @pl.when(cond)
            def _(e_on=e_on, first=first):
                if e_on:
                    epilogue_chunk()
                if first is not None:
                    k_step(first)

    if want_h:
        @pl.when(jnp.logical_and(has_prev, s == nc - 1))
        def _():
            os_ref[...] = lax.rsqrt(s2_ref[...] / n_total + EPS)


def _resid(a, w, x, g_post, g_next, *, tm=1024, tk=512, tc=512):
    m, k = a.shape
    n = w.shape[1]
    tm, tk, tc = _tile(m, tm), _tile(k, tk, LANES), _tile(n, tc, LANES)
    nt, nk, nc = m // tm, k // tk, n // tc
    assert nk >= nc, (nk, nc)
    want_h = g_next is not None
    if not want_h:
        g_next = g_post
    ti = lambda i: jnp.minimum(i, nt - 1)
    kk = lambda i, s: jnp.where(i < nt, s, nk - 1)
    pi = lambda i: jnp.maximum(i - 1, 0)
    cc = lambda i, s: jnp.where(i > 0, jnp.minimum(s, nc - 1), 0)
    out_shape = [jax.ShapeDtypeStruct((m, n), F32)]
    out_specs = [pl.BlockSpec((tm, tc), lambda i, s: (pi(i), cc(i, s)))]
    if want_h:
        out_shape += [jax.ShapeDtypeStruct((m, n), MXU_DTYPE), jax.ShapeDtypeStruct((m, 1), F32)]
        out_specs += [pl.BlockSpec((tm, tc), lambda i, s: (pi(i), cc(i, s))),
                      pl.BlockSpec((tm, 1), lambda i, s: (pi(i), 0))]
    outs = pl.pallas_call(
        functools.partial(_resid_kernel, nt=nt, nk=nk, nc=nc, tc=tc, want_h=want_h),
        out_shape=out_shape,
        grid=(nt + 1, nk),
        in_specs=[pl.BlockSpec((tm, tk), lambda i, s: (ti(i), kk(i, s))),
                  pl.BlockSpec((tk, n), lambda i, s: (kk(i, s), 0)),
                  pl.BlockSpec((tm, tc), lambda i, s: (pi(i), cc(i, s))),
                  pl.BlockSpec((1, tc), lambda i, s: (0, cc(i, s))),
                  pl.BlockSpec((1, tc), lambda i, s: (0, cc(i, s)))],
        out_specs=out_specs,
        scratch_shapes=[pltpu.VMEM((2, nc, tm, tc), F32), pltpu.VMEM((tm, 1), F32), pltpu.VMEM((tm, 1), F32)],
        compiler_params=_params("arbitrary", "arbitrary"),
        name="resid",
    )(a, w, x, g_post.reshape(1, n), g_next.reshape(1, n))
    return tuple(outs) if want_h else (outs[0], None, None)


def _hgrn_gates(uq, uf, lb):
    f = lb + (1.0 - lb) * jax.nn.sigmoid(uf)
    logf = jnp.log(jnp.maximum(f, F_FLOOR))
    q = jax.nn.silu(uq) * (HGRN_HEAD_DIM ** -0.5)
    return q, 1.0 - f, logf


def _hgrn_finish(o, ug, gain, dtype):
    o = o * _rms_scale(o) * gain
    return (o * jax.nn.silu(ug)).astype(dtype)


def _split3(x):
    h1 = x.astype(MXU_DTYPE)
    r = x - h1.astype(F32)
    h2 = r.astype(MXU_DTYPE)
    h3 = (r - h2.astype(F32)).astype(MXU_DTYPE)
    return h1, h2, h3


def _diag_offsets(q, k, cum, row, block):
    yield 0, jnp.sum(q * k, axis=-1, keepdims=True)
    for j in range(1, block):
        inside = (row % block) >= j
        d = jnp.where(inside, cum - pltpu.roll(cum, j, 0), NEG_BIG)
        yield j, jnp.sum(q * pltpu.roll(k, j, 0) * jnp.exp(d), axis=-1, keepdims=True)


def _diag_blocks(q, k, v, cum, tmp_ref):
    g = HGRN_DIAG
    n = q.shape[0] // g
    for i, a in enumerate((q, k, v, cum)):
        tmp_ref[i] = a
    views = [[tmp_ref[i, pl.ds(r, n, stride=g), :] for r in range(g)] for i in range(4)]
    qs, ks, vs, cs = views
    for r in range(g):
        acc = jnp.sum(qs[r] * ks[r], axis=-1, keepdims=True) * vs[r]
        for r2 in range(r):
            p = qs[r] * ks[r2] * jnp.exp(cs[r] - cs[r2])
            acc = acc + jnp.sum(p, axis=-1, keepdims=True) * vs[r2]
        tmp_ref[4, pl.ds(r, n, stride=g), :] = acc
    return tmp_ref[4]


def _hgrn_chunk(q, k, v, logf, st, tmp_ref):
    c = q.shape[0]
    row = lax.broadcasted_iota(jnp.int32, (c, 1), 0)
    rr = lax.broadcasted_iota(jnp.int32, (c, c), 0)
    cc = lax.broadcasted_iota(jnp.int32, (c, c), 1)
    tri = (cc <= rr).astype(MXU_DTYPE)
    cum = sum(jnp.dot(tri, h, preferred_element_type=F32) for h in _split3(logf)[::-1])

    o_diag = _diag_blocks(q, k, v, cum, tmp_ref)
    scores = jnp.zeros((c, c), F32)
    b = HGRN_DIAG
    while b < c:
        c3 = cum.reshape(c // (2 * b), 2 * b, cum.shape[1])
        e = jnp.exp(-jnp.abs(c3 - c3[:, b - 1:b, :])).reshape(cum.shape)
        odd = ((row // b) % 2) == 1
        qt = jnp.where(odd, q * e, 0.0).astype(MXU_DTYPE)
        kt = jnp.where(odd, 0.0, k * e).astype(MXU_DTYPE)
        s = lax.dot_general(qt, kt, (((1,), (1,)), ((), ())), preferred_element_type=F32)
        scores = scores + jnp.where((rr // (2 * b)) == (cc // (2 * b)), s, 0.0)
        b *= 2

    vm = v.astype(MXU_DTYPE)
    qd = (q * jnp.exp(cum)).astype(MXU_DTYPE)
    o = lax.dot_general(qd, st.astype(MXU_DTYPE), (((1,), (1,)), ((), ())), preferred_element_type=F32)
    o = o + jnp.dot(scores.astype(MXU_DTYPE), vm, preferred_element_type=F32) + o_diag
    last = cum[c - 1:c, :]
    kd = (k * jnp.exp(last - cum)).astype(MXU_DTYPE)
    st_new = st * jnp.exp(last) + jnp.dot(v.T.astype(MXU_DTYPE), kd, preferred_element_type=F32)
    return o, st_new


def _hgrn_prompt_kernel(uq_ref, uf_ref, ui_ref, ug_ref, lb_ref, gain_ref, o_ref, s_ref, st_ref, tmp_ref, *, chunk):
    n = pl.program_id(2)

    @pl.when(n == 0)
    def _():
        st_ref[...] = jnp.zeros(st_ref.shape, F32)

    dk = HGRN_HEAD_DIM
    for ci in range(uq_ref.shape[0] // chunk):
        rows = pl.ds(ci * chunk, chunk)
        for hh in range(uq_ref.shape[1] // dk):
            cols = slice(hh * dk, (hh + 1) * dk)
            q, k, logf = _hgrn_gates(uq_ref[rows, cols], uf_ref[rows, cols], lb_ref[:, cols])
            o, st_new = _hgrn_chunk(q, k, ui_ref[rows, cols], logf, st_ref[hh], tmp_ref.at[hh])
            st_ref[hh] = st_new
            o_ref[rows, cols] = _hgrn_finish(o, ug_ref[rows, cols], gain_ref[:, cols], o_ref.dtype)

    @pl.when(n == pl.num_programs(2) - 1)
    def _():
        for hh in range(st_ref.shape[0]):
            s_ref[hh] = st_ref[hh].T


def _hgrn_prompt(u, lb, gain, batch, seq, width, *, tb=512, hp=4):
    dk = HGRN_HEAD_DIM
    heads = width // dk
    chunk = min(HGRN_CHUNK, seq)
    tb, hp = _tile(seq, tb), _tile(heads, hp)
    assert tb % chunk == 0
    nt, hg = seq // tb, heads // hp
    u_spec = lambda c: pl.BlockSpec((tb, hp * dk), lambda b, h, n: (b * nt + n, c * hg + h))
    vec_spec = pl.BlockSpec((1, hp * dk), lambda b, h, n: (0, h))
    return pl.pallas_call(
        functools.partial(_hgrn_prompt_kernel, chunk=chunk),
        out_shape=[jax.ShapeDtypeStruct((batch * seq, width), MXU_DTYPE),
                   jax.ShapeDtypeStruct((batch, heads, dk, dk), F32)],
        grid=(batch, hg, nt),
        in_specs=[u_spec(0), u_spec(1), u_spec(2), u_spec(3), vec_spec, vec_spec],
        out_specs=[pl.BlockSpec((tb, hp * dk), lambda b, h, n: (b * nt + n, h)),
                   pl.BlockSpec((None, hp, dk, dk), lambda b, h, n: (b, h, 0, 0))],
        scratch_shapes=[pltpu.VMEM((hp, dk, dk), F32), pltpu.VMEM((hp, 5, chunk, dk), F32)],
        compiler_params=_params("parallel", "parallel", "arbitrary"),
        name="hgrn_prompt",
    )(u, u, u, u, lb.reshape(1, width), gain.reshape(1, width))


def _hgrn_sample_kernel(uq_ref, uf_ref, ui_ref, ug_ref, lb_ref, gain_ref, s0_ref, *rest, t):
    o_ref, s_ref = rest[-2:]
    n_prev = s_ref.shape[0] - 1
    for i in range(n_prev):
        s_ref[i] = rest[0][i]
    nb = s0_ref.shape[0]
    width = uq_ref.shape[1]
    dk = HGRN_HEAD_DIM
    heads = width // dk
    pad = dk - 2 * t
    assert pad >= 0
    lb = lb_ref[...]
    row = lax.broadcasted_iota(jnp.int32, (t, 1), 0)
    outs = [[] for _ in range(heads)]
    for sq in range(nb):
        rows = pl.ds(sq * t, t)
        q, k, logf = _hgrn_gates(uq_ref[rows, :], uf_ref[rows, :], lb)
        v = ui_ref[rows, :]
        cum = logf
        sh = 1
        while sh < t:
            cum = cum + jnp.where(row >= sh, pltpu.roll(cum, sh, 0), 0.0)
            sh *= 2
        last = cum[t - 1:t, :]
        qd = q * jnp.exp(cum)
        kd = k * jnp.exp(last - cum)
        dec = jnp.exp(last)
        for h in range(heads):
            cols = slice(h * dk, (h + 1) * dk)
            vh = v[:, cols]
            s0 = s0_ref[sq, h]
            o = jnp.dot(qd[:, cols].astype(MXU_DTYPE), s0.astype(MXU_DTYPE), preferred_element_type=F32)
            for j, rs in _diag_offsets(q[:, cols], k[:, cols], cum[:, cols], row, t):
                o = o + rs * (vh if j == 0 else pltpu.roll(vh, j, 0))
            outs[h].append(o)
            stack_t = jnp.concatenate([kd[:, cols], jnp.broadcast_to(dec[:, cols], (t, dk)),
                                       jnp.zeros((pad, dk), F32)], axis=0).T
            v_pad = jnp.concatenate([vh, jnp.zeros((dk - t, dk), F32)], axis=0)
            s_ref[n_prev, sq, h] = stack_t[:, t:t + 1] * s0 + jnp.dot(stack_t.astype(MXU_DTYPE), v_pad.astype(MXU_DTYPE),
                                                               preferred_element_type=F32)
    for h in range(heads):
        cols = slice(h * dk, (h + 1) * dk)
        o = jnp.concatenate(outs[h], axis=0)
        o_ref[:, cols] = _hgrn_finish(o, ug_ref[:, cols], gain_ref[:, cols], o_ref.dtype)


def _hgrn_sample(u, lb, gain, s0, l, prev, row0, batch, t, width, *, nb=2):
    dk = HGRN_HEAD_DIM
    heads = width // dk
    nb = _tile(batch, nb)
    assert row0 % (nb * t) == 0 and (prev is None) == (l == 0)
    r0 = row0 // (nb * t)
    u_spec = lambda c: pl.BlockSpec((nb * t, width), lambda b: (r0 + b, c))
    vec_spec = pl.BlockSpec((1, width), lambda b: (0, 0))
    stack_spec = lambda n: pl.BlockSpec((n, nb, heads, dk, dk), lambda b: (0, b, 0, 0, 0))
    in_specs = [u_spec(0), u_spec(1), u_spec(2), u_spec(3), vec_spec, vec_spec,
                pl.BlockSpec((None, nb, heads, dk, dk), lambda b: (l, b, 0, 0, 0))]
    args = [u, u, u, u, lb.reshape(1, width), gain.reshape(1, width), s0]
    if prev is not None:
        in_specs.append(stack_spec(l))
        args.append(prev)
    return pl.pallas_call(
        functools.partial(_hgrn_sample_kernel, t=t),
        out_shape=[jax.ShapeDtypeStruct((batch * t, width), MXU_DTYPE),
                   jax.ShapeDtypeStruct((l + 1, batch, heads, dk, dk), F32)],
        grid=(batch // nb,),
        in_specs=in_specs,
        out_specs=[pl.BlockSpec((nb * t, width), lambda b: (b, 0)), stack_spec(l + 1)],
        compiler_params=_params("parallel"),
        name="hgrn_sample",
    )(*args)


def _pool_kernel(p_ref, buf_ref, w_ref, scale_ref, o_ref, nbuf_ref, ext_ref, *, n_past):
    n = pl.program_id(1)
    nb, tb, width = p_ref.shape
    gw = width // len(POOL_WINDOWS)
    hist = POOL_HIST

    @pl.when(n == 0)
    def _():
        ext_ref[:, 0:hist, :] = buf_ref[...]

    @pl.when(n > 0)
    def _():
        ext_ref[:, 0:hist, :] = ext_ref[:, tb:tb + hist, :]

    ext_ref[:, hist:hist + tb, :] = p_ref[...]
    pos = n_past + n * tb + lax.broadcasted_iota(jnp.int32, (1, tb, 1), 1)
    for g, win in enumerate(POOL_WINDOWS):
        cols = slice(g * gw, (g + 1) * gw)
        tot = ext_ref[:, hist:hist + tb, cols]
        x = tot
        for i in range(1, win):
            tot = tot + ext_ref[:, hist - i:hist - i + tb, cols]
        count = jnp.minimum(pos + 1, win).astype(F32)
        pooled = (tot / count - x).reshape(nb * tb, gw)
        mixed = jnp.dot(pooled.astype(MXU_DTYPE), w_ref[g].astype(MXU_DTYPE), preferred_element_type=F32)
        o_ref[:, cols] = (mixed * scale_ref[:, cols]).astype(o_ref.dtype)

    @pl.when(n == pl.num_programs(1) - 1)
    def _():
        nbuf_ref[...] = ext_ref[:, tb:tb + hist, :]


def _pool(u, buf, w_grp, l, scale, col, row0, batch, seq, n_past, *, nb, tb):
    _, groups, gw, _ = w_grp.shape
    width = groups * gw
    hist = POOL_HIST
    nb, tb = _tile(batch, nb), _tile(seq, tb)
    nt = seq // tb
    assert col % width == 0 and row0 % (nb * tb) == 0 and (nb == 1 or nt == 1) and tb % 8 == 0
    if buf is None:
        buf = jnp.zeros((batch, hist, width), F32)
    else:
        buf = jnp.pad(buf, ((0, 0), (1, 0), (0, 0)))
    u3 = u.reshape(u.shape[0] // tb, tb, u.shape[1])
    r0 = row0 // (nb * tb)
    mixed, nbuf = pl.pallas_call(
        functools.partial(_pool_kernel, n_past=n_past),
        out_shape=[jax.ShapeDtypeStruct((batch * seq, width), MXU_DTYPE),
                   jax.ShapeDtypeStruct((batch, hist, width), F32)],
        grid=(batch // nb, nt),
        in_specs=[pl.BlockSpec((nb, tb, width), lambda b, n: (r0 + b * nt + n, 0, col // width)),
                  pl.BlockSpec((nb, hist, width), lambda b, n: (b, 0, 0)),
                  pl.BlockSpec((None, groups, gw, gw), lambda b, n: (l, 0, 0, 0)),
                  pl.BlockSpec((1, width), lambda b, n: (0, 0))],
        out_specs=[pl.BlockSpec((nb * tb, width), lambda b, n: (b * nt + n, 0)),
                   pl.BlockSpec((nb, hist, width), lambda b, n: (b, 0, 0))],
        scratch_shapes=[pltpu.VMEM((nb, hist + tb, width), F32)],
        compiler_params=_params("parallel", "arbitrary"),
        name="pool",
    )(u3, buf, w_grp, scale.reshape(1, width))
    return mixed, nbuf[:, 1:, :]


def _attend_kernel(q_ref, k_ref, v_ref, o_ref, *, tq, hd):
    nb = k_ref.shape[0]
    for b in range(nb):
        rows = slice(b * tq, (b + 1) * tq)
        for h in range(q_ref.shape[1] // hd):
            cols = slice(h * hd, (h + 1) * hd)
            k, v = k_ref[b, :, cols], v_ref[b, :, cols]
            q = q_ref[rows, cols].astype(MXU_DTYPE)
            s = lax.dot_general(q, k.astype(MXU_DTYPE), (((1,), (1,)), ((), ())),
                                preferred_element_type=F32) * (hd ** -0.5)
            e = jnp.exp(s - jnp.max(s, axis=-1, keepdims=True))
            a = e / jnp.sum(e, axis=-1, keepdims=True)
            o_ref[rows, cols] = jnp.dot(a.astype(MXU_DTYPE), v.astype(MXU_DTYPE),
                                        preferred_element_type=F32).astype(o_ref.dtype)


def _attend(u, k_arr, v_arr, k_spec, v_spec, hd, hps, col, row0, batch, seq, *, nb, tq):
    nb, tq = _tile(batch, nb), _tile(seq, tq)
    nt = seq // tq
    wq = hps * hd
    assert col % wq == 0 and row0 % (nb * tq) == 0 and (nb == 1 or nt == 1) and MEM_HEADS % hps == 0
    r0, c0 = row0 // (nb * tq), col // wq
    return pl.pallas_call(
        functools.partial(_attend_kernel, tq=tq, hd=hd),
        out_shape=jax.ShapeDtypeStruct((batch * seq, MEM_HEADS * hd), MXU_DTYPE),
        grid=(batch // nb, nt, MEM_HEADS // hps),
        in_specs=[pl.BlockSpec((nb * tq, wq), lambda b, n, h: (r0 + b * nt + n, c0 + h)), k_spec(nb), v_spec(nb)],
        out_specs=pl.BlockSpec((nb * tq, wq), lambda b, n, h: (b * nt + n, h)),
        compiler_params=_params("parallel", "arbitrary", "arbitrary"),
        name="attend",
    )(u, k_arr, v_arr)


LANES = 128


def _cache_rows(cache):
    depth, batch, tokens, heads, hd = cache.shape
    lt = hd // LANES
    c = cache.reshape(depth, batch, tokens, heads, lt, LANES).transpose(0, 1, 2, 4, 3, 5)
    return c.reshape(depth, batch, tokens * lt * heads, LANES)


def _attend_rows_kernel(q_ref, k_ref, v_ref, o_ref, *, tq, hd):
    nb, n_rows, _ = k_ref.shape
    nlt = hd // LANES
    rper = MEM_HEADS * nlt
    nq = MEM_HEADS * tq
    row = lax.broadcasted_iota(jnp.int32, (nq, n_rows), 0)
    col = lax.broadcasted_iota(jnp.int32, (nq, n_rows), 1)
    valid = (col % rper) == (row // tq)
    for b in range(nb):
        rows = slice(b * tq, (b + 1) * tq)
        qst = jnp.concatenate([q_ref[rows, h * hd + lt * LANES:h * hd + (lt + 1) * LANES]
                               for lt in range(nlt) for h in range(MEM_HEADS)], axis=0).astype(MXU_DTYPE)
        p = lax.dot_general(qst, k_ref[b].astype(MXU_DTYPE), (((1,), (1,)), ((), ())), preferred_element_type=F32)
        s = p[0:nq]
        for lt in range(1, nlt):
            s = s + pltpu.roll(p[lt * nq:(lt + 1) * nq], n_rows - lt * MEM_HEADS, 1)
        s = jnp.where(valid, s * (hd ** -0.5), NEG_BIG)
        e = jnp.exp(s - jnp.max(s, axis=-1, keepdims=True))
        a = e / jnp.sum(e, axis=-1, keepdims=True)
        ast = jnp.concatenate([a if lt == 0 else pltpu.roll(a, lt * MEM_HEADS, 1) for lt in range(nlt)], axis=0)
        o = jnp.dot(ast.astype(MXU_DTYPE), v_ref[b].astype(MXU_DTYPE), preferred_element_type=F32)
        for lt in range(nlt):
            for h in range(MEM_HEADS):
                r0 = lt * nq + h * tq
                o_ref[rows, h * hd + lt * LANES:h * hd + (lt + 1) * LANES] = o[r0:r0 + tq].astype(o_ref.dtype)


def _attend_rows(u, k_rows, v_rows, l, hd, col, row0, batch, tq, *, nb):
    width = MEM_HEADS * hd
    nb = _tile(batch, nb)
    n_rows = k_rows.shape[2]
    assert col % width == 0 and row0 % (nb * tq) == 0 and hd % LANES == 0
    r0, c0 = row0 // (nb * tq), col // width
    kv_spec = pl.BlockSpec((None, nb, n_rows, LANES), lambda b: (l, b, 0, 0))
    return pl.pallas_call(
        functools.partial(_attend_rows_kernel, tq=tq, hd=hd),
        out_shape=jax.ShapeDtypeStruct((batch * tq, width), MXU_DTYPE),
        grid=(batch // nb,),
        in_specs=[pl.BlockSpec((nb * tq, width), lambda b: (r0 + b, c0)), kv_spec, kv_spec],
        out_specs=pl.BlockSpec((nb * tq, width), lambda b: (b, 0)),
        compiler_params=_params("parallel"),
        name="attend_rows",
    )(u, k_rows, v_rows)


PAST_LEN = 16384


def kernel(x_prompt, x_sample, state_hgrn, state_pool, cache_mem_k, cache_mem_v, mem_prompt, norm_pre_mix, norm_post_mix, norm_pre_mlp, norm_post_mlp, norm_mem, w_in, hgrn_lb, hgrn_out_norm, w_pool, pool_scale, w_mem_kv, w_branch_hgrn, w_branch_pool, w_branch_mem, w_out, w_up, w_down):
    bp, sp, d = x_prompt.shape
    bs, ss, _ = x_sample.shape
    depth = w_in.shape[0]
    hw, pw, mw = hgrn_lb.shape[1], pool_scale.shape[1], w_branch_mem.shape[1]
    mp, ms = bp * sp, bs * ss
    tokens = mem_prompt.shape[1]
    heads = hw // HGRN_HEAD_DIM
    pool_col, mem_col, gate_col = 4 * hw, 4 * hw + pw, 4 * hw + pw + mw

    lb_soft = jax.nn.softmax(hgrn_lb.astype(F32), axis=0)
    lb_all = jnp.cumsum(lb_soft, axis=0) - lb_soft[0:1]

    x = jnp.concatenate([x_prompt.reshape(mp, d), x_sample.reshape(ms, d)], axis=0)
    mem2 = mem_prompt.reshape(bp * tokens, d)
    h, h_scale = _rmsnorm(x, norm_pre_mix[0]), None
    hd = mw // MEM_HEADS
    if hd % LANES == 0:
        k_rows, v_rows = _cache_rows(cache_mem_k), _cache_rows(cache_mem_v)
    outs = [[] for _ in range(5)]
    st_s = None
    for l in range(depth):
        kv = _proj(_rmsnorm(mem2, norm_mem[l]), w_mem_kv, l, out_dtype=F32, tm=1024)
        kv3 = kv.reshape(bp, tokens, 2 * mw)
        u, w_out_l = _proj(h, w_in, l, h_scale, out_dtype=F32, cast_src=w_out)

        o_p, st_p = _hgrn_prompt(u, lb_all[l], hgrn_out_norm[l], bp, sp, hw)
        o_s, st_s = _hgrn_sample(u, lb_all[l], hgrn_out_norm[l], state_hgrn, l, st_s, mp, bs, ss, hw)
        pool_p, buf_p = _pool(u, None, w_pool, l, pool_scale[l], pool_col, 0, bp, sp, 0, nb=1, tb=512)
        pool_s, buf_s = _pool(u, state_pool[l], w_pool, l, pool_scale[l], pool_col, mp, bs, ss, PAST_LEN,
                              nb=32, tb=ss)
        kv_spec = lambda c: lambda nb: pl.BlockSpec((nb, tokens, mw), lambda b, n, h: (b, 0, c))
        mem_p = _attend(u, kv3, kv3, kv_spec(0), kv_spec(1), hd, MEM_HEADS, mem_col, 0, bp, sp, nb=1, tq=512)
        if hd % LANES == 0:
            mem_s = _attend_rows(u, k_rows, v_rows, l, hd, mem_col, mp, bs, ss, nb=4)
        else:
            mem_s = _attend(u, cache_mem_k[l].reshape(bs, tokens, mw), cache_mem_v[l].reshape(bs, tokens, mw),
                            kv_spec(0), kv_spec(0), hd, MEM_HEADS, mem_col, mp, bs, ss, nb=4, tq=ss)

        merged = _merge((o_p, pool_p, mem_p), (o_s, pool_s, mem_s), w_branch_hgrn, w_branch_pool, w_branch_mem, l,
                        u, gate_col)
        x, h2, h2_scale = _resid(merged, w_out_l, x, norm_post_mix[l], norm_pre_mlp[l])
        ff, w_down_l = _proj(h2, w_up, l, h2_scale, out_dtype=MXU_DTYPE, relu2=True, cast_src=w_down)
        x, h, h_scale = _resid(ff, w_down_l, x, norm_post_mlp[l], norm_pre_mix[l + 1] if l + 1 < depth else None)

        for lst, val in zip(outs, (st_p, buf_p, kv3[..., :mw].reshape(bp, tokens, MEM_HEADS, hd),
                                   kv3[..., mw:].reshape(bp, tokens, MEM_HEADS, hd), buf_s)):
            lst.append(val)

    stacked = [jnp.stack(o) for o in outs]
    return (x[:mp].reshape(bp, sp, d), x[mp:].reshape(bs, ss, d), *stacked[:4], st_s, stacked[4])
```

```python
import functools
import math

import jax
import jax.numpy as jnp
from jax import lax
from jax.experimental import pallas as pl
from jax.experimental.pallas import tpu as pltpu

F32 = jnp.float32
MXU_DTYPE = jnp.bfloat16
EPS = 1e-6
F_FLOOR = 1e-30
NEG_BIG = -1e30
HGRN_HEAD_DIM = 128
POOL_WINDOWS = (2, 4, 8, 16)
POOL_HIST = 16
MEM_HEADS = 4
N_BRANCHES = 3
V7X_VMEM_LIMIT_BYTES = 56 * 1024 * 1024
HGRN_CHUNK = 128
HGRN_DIAG = 8


def _params(*sem):
    return pltpu.CompilerParams(dimension_semantics=sem, vmem_limit_bytes=V7X_VMEM_LIMIT_BYTES)


def _tile(n, t, quantum=1):
    if n <= t:
        return n
    t -= t % quantum
    while n % t:
        t -= quantum
    assert t > 0, (n, quantum)
    return t


def _rms_scale(x):
    return lax.rsqrt(jnp.mean(x * x, axis=-1, keepdims=True) + EPS)


def _rmsnorm_kernel(x_ref, g_ref, o_ref):
    x = x_ref[...]
    o_ref[...] = (x * _rms_scale(x) * g_ref[...]).astype(o_ref.dtype)


def _rmsnorm(x, g, tm=256):
    m, d = x.shape
    tm = _tile(m, tm)
    return pl.pallas_call(
        _rmsnorm_kernel,
        out_shape=jax.ShapeDtypeStruct((m, d), MXU_DTYPE),
        grid=(m // tm,),
        in_specs=[pl.BlockSpec((tm, d), lambda i: (i, 0)), pl.BlockSpec((1, d), lambda i: (0, 0))],
        out_specs=pl.BlockSpec((tm, d), lambda i: (i, 0)),
        compiler_params=_params("parallel"),
        name="rmsnorm",
    )(x, g.reshape(1, d))


def _proj_kernel(*refs, relu2, has_scale, casts):
    n_in = 2 + has_scale
    srcs = refs[n_in:n_in + len(casts)]
    o_ref = refs[n_in + len(casts)]
    dsts = refs[n_in + len(casts) + 1:]
    t = pl.program_id(0) * pl.num_programs(1) + pl.program_id(1)
    for (start, count), src_ref, dst_ref in zip(casts, srcs, dsts):
        @pl.when(jnp.logical_and(t >= start, t < start + count))
        def _(src_ref=src_ref, dst_ref=dst_ref):
            dst_ref[...] = src_ref[...].astype(dst_ref.dtype)

    a_ref, w_ref = refs[:2]
    acc = jnp.dot(a_ref[...], w_ref[...].astype(MXU_DTYPE), preferred_element_type=F32)
    if has_scale:
        acc = acc * refs[2][...]
    if relu2:
        acc = jnp.square(jnp.maximum(acc, 0.0))
    o_ref[...] = acc.astype(o_ref.dtype)


def _proj(a, w, l, row_scale=None, *, out_dtype, relu2=False, casts=(), tm=2304, tn=512):
    m, k = a.shape
    n = w.shape[-1]
    tm, tn = _tile(m, tm), _tile(n, tn, LANES)
    nj = n // tn
    steps = (m // tm) * nj
    w_spec = (pl.BlockSpec((k, tn), lambda i, j: (0, j)) if w.ndim == 2
              else pl.BlockSpec((None, k, tn), lambda i, j: (l, 0, j)))
    in_specs = [pl.BlockSpec((tm, k), lambda i, j: (i, 0), pipeline_mode=pl.Buffered(1)), w_spec]
    args = [a, w]
    if row_scale is not None:
        in_specs.append(pl.BlockSpec((tm, 1), lambda i, j: (i, 0)))
        args.append(row_scale)
    out_shape = [jax.ShapeDtypeStruct((m, n), out_dtype)]
    out_specs = [pl.BlockSpec((tm, tn), lambda i, j: (i, j))]
    schedule = []
    for group in casts:
        rows = [min(src.shape[1], max(8, 1 << ((1 << 18) // src.shape[2]).bit_length() - 1)) for src, _ in group]
        while sum(-(-src.shape[1] // rw) for (src, _), rw in zip(group, rows)) > steps:
            rows = [min(src.shape[1], 2 * rw) for (src, _), rw in zip(group, rows)]
        start = 0
        for (src, ls), rw in zip(group, rows):
            _, r, c = src.shape
            assert r % rw == 0, (r, rw)
            count = r // rw
            blk = lambda i, j, start=start, count=count: jnp.clip(i * nj + j - start, 0, count - 1)
            in_specs.append(pl.BlockSpec((None, rw, c), lambda i, j, ls=ls, blk=blk: (ls, blk(i, j), 0)))
            args.append(src)
            out_shape.append(jax.ShapeDtypeStruct((r, c), MXU_DTYPE))
            out_specs.append(pl.BlockSpec((rw, c), lambda i, j, blk=blk: (blk(i, j), 0)))
            schedule.append((start, count))
            start += count
    outs = pl.pallas_call(
        functools.partial(_proj_kernel, relu2=relu2, has_scale=row_scale is not None, casts=tuple(schedule)),
        out_shape=out_shape,
        grid=(m // tm, nj),
        in_specs=in_specs,
        out_specs=out_specs,
        compiler_params=_params("arbitrary", "arbitrary"),
        name="proj",
    )(*args)
    return (outs[0], list(outs[1:])) if casts else outs[0]


def _merge_kernel(*refs, n_first):
    w_refs, g_refs, o_ref = refs[6:9], refs[9:12], refs[12]

    def merged(a_refs):
        tot = None
        for a_ref, w_ref, g_ref in zip(a_refs, w_refs, g_refs):
            y = jnp.dot(a_ref[...], w_ref[...].astype(MXU_DTYPE), preferred_element_type=F32)
            y = jax.nn.sigmoid(g_ref[...]) * y
            tot = y if tot is None else tot + y
        o_ref[...] = tot.astype(o_ref.dtype)

    first = pl.program_id(0) < n_first
    pl.when(first)(lambda: merged(refs[0:3]))
    pl.when(jnp.logical_not(first))(lambda: merged(refs[3:6]))


def _merge(branches_a, branches_b, w_h, w_p, w_m, l, u, gate_col, *, tm=1024, tn=512):
    ma, mb = branches_a[0].shape[0], branches_b[0].shape[0]
    d = w_h.shape[2]
    tm = _tile(mb, _tile(ma, tm))
    tn = math.gcd(_tile(d, tn, LANES), gate_col)
    gc, nd, na = gate_col // tn, d // tn, ma // tm
    once = pl.Buffered(1)
    a_spec = lambda a: pl.BlockSpec((tm, a.shape[1]), lambda i, j: (jnp.minimum(i, na - 1), 0), pipeline_mode=once)
    b_spec = lambda a: pl.BlockSpec((tm, a.shape[1]), lambda i, j: (jnp.maximum(i - na, 0), 0), pipeline_mode=once)
    w_spec = lambda w: pl.BlockSpec((None, w.shape[1], tn), lambda i, j: (l, 0, j))
    g_spec = lambda b: pl.BlockSpec((tm, tn), lambda i, j: (i, gc + b * nd + j))
    return pl.pallas_call(
        functools.partial(_merge_kernel, n_first=na),
        out_shape=jax.ShapeDtypeStruct((ma + mb, d), MXU_DTYPE),
        grid=((ma + mb) // tm, nd),
        in_specs=[a_spec(a) for a in branches_a] + [b_spec(a) for a in branches_b]
                 + [w_spec(w_h), w_spec(w_p), w_spec(w_m), g_spec(0), g_spec(1), g_spec(2)],
        out_specs=pl.BlockSpec((tm, tn), lambda i, j: (i, j)),
        compiler_params=_params("parallel", "arbitrary"),
        name="merge",
    )(*branches_a, *branches_b, w_h, w_p, w_m, u, u, u)


def _resid_kernel(a_ref, w_ref, x_ref, gpost_ref, gnext_ref, ox_ref, *rest, nt, nk, nc, tc, want_h):
    if want_h:
        oh_ref, os_ref, acc_ref, s1_ref, s2_ref = rest
    else:
        acc_ref, s1_ref, s2_ref = rest
    i, s = pl.program_id(0), pl.program_id(1)
    cur = i % 2
    prev = 1 - cur
    n_total = nc * tc
    has_prev = i > 0

    @pl.when(jnp.logical_and(has_prev, s == 0))
    def _():
        ss = jnp.zeros(s1_ref.shape, F32)
        for c in range(nc):
            y = acc_ref[prev, c]
            ss = ss + jnp.sum(y * y, axis=-1, keepdims=True)
        s1_ref[...] = lax.rsqrt(ss / n_total + EPS)
        s2_ref[...] = jnp.zeros(s2_ref.shape, F32)

    def epilogue_chunk():
        xn = x_ref[...] + acc_ref[prev, s] * s1_ref[...] * gpost_ref[...]
        ox_ref[...] = xn
        if want_h:
            oh_ref[...] = (xn * gnext_ref[...]).astype(oh_ref.dtype)
            s2_ref[...] += jnp.sum(xn * xn, axis=-1, keepdims=True)

    def k_step(first):
        a = a_ref[...]
        w = w_ref[...].astype(MXU_DTYPE)
        for c in range(nc):
            y = jnp.dot(a, w[:, c * tc:(c + 1) * tc], preferred_element_type=F32)
            if first:
                acc_ref[cur, c] = y
            else:
                acc_ref[cur, c] += y

    do_e = jnp.logical_and(has_prev, s < nc)
    do_k = i < nt
    for e_on in (True, False):
        for first in (True, False, None):
            cond = do_e if e_on else jnp.logical_not(do_e)
            if first is None:
                cond = jnp.logical_and(cond, jnp.logical_not(do_k))
            else:
                cond = jnp.logical_and(jnp.logical_and(cond, do_k), (s == 0) if first else (s > 0))
            if not e_on and first is None:
                continue

            @pl.when(cond)
            def _(e_on=e_on, first=first):
                if e_on:
                    epilogue_chunk()
                if first is not None:
                    k_step(first)

    if want_h:
        @pl.when(jnp.logical_and(has_prev, s == nc - 1))
        def _():
            os_ref[...] = lax.rsqrt(s2_ref[...] / n_total + EPS)


def _resid(a, w, x, g_post, g_next, *, tm=1024, tk=512, tc=512):
    m, k = a.shape
    n = w.shape[1]
    tm, tk, tc = _tile(m, tm), _tile(k, tk, LANES), _tile(n, tc, LANES)
    nt, nk, nc = m // tm, k // tk, n // tc
    assert nk >= nc, (nk, nc)
    want_h = g_next is not None
    if not want_h:
        g_next = g_post
    ti = lambda i: jnp.minimum(i, nt - 1)
    kk = lambda i, s: jnp.where(i < nt, s, nk - 1)
    pi = lambda i: jnp.maximum(i - 1, 0)
    cc = lambda i, s: jnp.where(i > 0, jnp.minimum(s, nc - 1), 0)
    out_shape = [jax.ShapeDtypeStruct((m, n), F32)]
    out_specs = [pl.BlockSpec((tm, tc), lambda i, s: (pi(i), cc(i, s)))]
    if want_h:
        out_shape += [jax.ShapeDtypeStruct((m, n), MXU_DTYPE), jax.ShapeDtypeStruct((m, 1), F32)]
        out_specs += [pl.BlockSpec((tm, tc), lambda i, s: (pi(i), cc(i, s))),
                      pl.BlockSpec((tm, 1), lambda i, s: (pi(i), 0))]
    outs = pl.pallas_call(
        functools.partial(_resid_kernel, nt=nt, nk=nk, nc=nc, tc=tc, want_h=want_h),
        out_shape=out_shape,
        grid=(nt + 1, nk),
        in_specs=[pl.BlockSpec((tm, tk), lambda i, s: (ti(i), kk(i, s))),
                  pl.BlockSpec((tk, n), lambda i, s: (kk(i, s), 0)),
                  pl.BlockSpec((tm, tc), lambda i, s: (pi(i), cc(i, s))),
                  pl.BlockSpec((1, tc), lambda i, s: (0, cc(i, s))),
                  pl.BlockSpec((1, tc), lambda i, s: (0, cc(i, s)))],
        out_specs=out_specs,
        scratch_shapes=[pltpu.VMEM((2, nc, tm, tc), F32), pltpu.VMEM((tm, 1), F32), pltpu.VMEM((tm, 1), F32)],
        compiler_params=_params("arbitrary", "arbitrary"),
        name="resid",
    )(a, w, x, g_post.reshape(1, n), g_next.reshape(1, n))
    return tuple(outs) if want_h else (outs[0], None, None)


def _hgrn_gates(uq, uf, lb):
    f = lb + (1.0 - lb) * jax.nn.sigmoid(uf)
    logf = jnp.log(jnp.maximum(f, F_FLOOR))
    q = jax.nn.silu(uq) * (HGRN_HEAD_DIM ** -0.5)
    return q, 1.0 - f, logf


def _hgrn_finish(o, ug, gain, dtype):
    o = o * _rms_scale(o) * gain
    return (o * jax.nn.silu(ug)).astype(dtype)


def _split3(x):
    h1 = x.astype(MXU_DTYPE)
    r = x - h1.astype(F32)
    h2 = r.astype(MXU_DTYPE)
    h3 = (r - h2.astype(F32)).astype(MXU_DTYPE)
    return h1, h2, h3


def _diag_offsets(q, k, cum, row, block):
    yield 0, jnp.sum(q * k, axis=-1, keepdims=True)
    for j in range(1, block):
        inside = (row % block) >= j
        d = jnp.where(inside, cum - pltpu.roll(cum, j, 0), NEG_BIG)
        yield j, jnp.sum(q * pltpu.roll(k, j, 0) * jnp.exp(d), axis=-1, keepdims=True)


def _diag_blocks(q, k, v, cum, tmp_ref):
    g = HGRN_DIAG
    n = q.shape[0] // g
    for i, a in enumerate((q, k, v, cum)):
        tmp_ref[i] = a
    views = [[tmp_ref[i, pl.ds(r, n, stride=g), :] for r in range(g)] for i in range(4)]
    qs, ks, vs, cs = views
    for r in range(g):
        acc = jnp.sum(qs[r] * ks[r], axis=-1, keepdims=True) * vs[r]
        for r2 in range(r):
            p = qs[r] * ks[r2] * jnp.exp(cs[r] - cs[r2])
            acc = acc + jnp.sum(p, axis=-1, keepdims=True) * vs[r2]
        tmp_ref[4, pl.ds(r, n, stride=g), :] = acc
    return tmp_ref[4]


def _hgrn_chunk(q, k, v, logf, st, tmp_ref):
    c = q.shape[0]
    row = lax.broadcasted_iota(jnp.int32, (c, 1), 0)
    rr = lax.broadcasted_iota(jnp.int32, (c, c), 0)
    cc = lax.broadcasted_iota(jnp.int32, (c, c), 1)
    tri = (cc <= rr).astype(MXU_DTYPE)
    cum = sum(jnp.dot(tri, h, preferred_element_type=F32) for h in _split3(logf)[::-1])

    o_diag = _diag_blocks(q, k, v, cum, tmp_ref)
    scores = jnp.zeros((c, c), F32)
    b = HGRN_DIAG
    while b < c:
        c3 = cum.reshape(c // (2 * b), 2 * b, cum.shape[1])
        e = jnp.exp(-jnp.abs(c3 - c3[:, b - 1:b, :])).reshape(cum.shape)
        odd = ((row // b) % 2) == 1
        qt = jnp.where(odd, q * e, 0.0).astype(MXU_DTYPE)
        kt = jnp.where(odd, 0.0, k * e).astype(MXU_DTYPE)
        s = lax.dot_general(qt, kt, (((1,), (1,)), ((), ())), preferred_element_type=F32)
        scores = scores + jnp.where((rr // (2 * b)) == (cc // (2 * b)), s, 0.0)
        b *= 2

    vm = v.astype(MXU_DTYPE)
    qd = (q * jnp.exp(cum)).astype(MXU_DTYPE)
    o = lax.dot_general(qd, st.astype(MXU_DTYPE), (((1,), (1,)), ((), ())), preferred_element_type=F32)
    o = o + jnp.dot(scores.astype(MXU_DTYPE), vm, preferred_element_type=F32) + o_diag
    last = cum[c - 1:c, :]
    kd = (k * jnp.exp(last - cum)).astype(MXU_DTYPE)
    st_new = st * jnp.exp(last) + jnp.dot(v.T.astype(MXU_DTYPE), kd, preferred_element_type=F32)
    return o, st_new


def _hgrn_prompt_kernel(uq_ref, uf_ref, ui_ref, ug_ref, lb_ref, gain_ref, o_ref, s_ref, st_ref, tmp_ref, *, chunk):
    n = pl.program_id(2)

    @pl.when(n == 0)
    def _():
        st_ref[...] = jnp.zeros(st_ref.shape, F32)

    dk = HGRN_HEAD_DIM
    for ci in range(uq_ref.shape[0] // chunk):
        rows = pl.ds(ci * chunk, chunk)
        for hh in range(uq_ref.shape[1] // dk):
            cols = slice(hh * dk, (hh + 1) * dk)
            q, k, logf = _hgrn_gates(uq_ref[rows, cols], uf_ref[rows, cols], lb_ref[:, cols])
            o, st_new = _hgrn_chunk(q, k, ui_ref[rows, cols], logf, st_ref[hh], tmp_ref.at[hh])
            st_ref[hh] = st_new
            o_ref[rows, cols] = _hgrn_finish(o, ug_ref[rows, cols], gain_ref[:, cols], o_ref.dtype)

    @pl.when(n == pl.num_programs(2) - 1)
    def _():
        for hh in range(st_ref.shape[0]):
            s_ref[hh] = st_ref[hh].T


def _hgrn_prompt(u, lb, gain, batch, seq, width, *, tb=512, hp=4):
    dk = HGRN_HEAD_DIM
    heads = width // dk
    chunk = min(HGRN_CHUNK, seq)
    tb, hp = _tile(seq, tb), _tile(heads, hp)
    assert tb % chunk == 0
    nt, hg = seq // tb, heads // hp
    u_spec = lambda c: pl.BlockSpec((tb, hp * dk), lambda b, h, n: (b * nt + n, c * hg + h))
    vec_spec = pl.BlockSpec((1, hp * dk), lambda b, h, n: (0, h))
    return pl.pallas_call(
        functools.partial(_hgrn_prompt_kernel, chunk=chunk),
        out_shape=[jax.ShapeDtypeStruct((batch * seq, width), MXU_DTYPE),
                   jax.ShapeDtypeStruct((batch, heads, dk, dk), F32)],
        grid=(batch, hg, nt),
        in_specs=[u_spec(0), u_spec(1), u_spec(2), u_spec(3), vec_spec, vec_spec],
        out_specs=[pl.BlockSpec((tb, hp * dk), lambda b, h, n: (b * nt + n, h)),
                   pl.BlockSpec((None, hp, dk, dk), lambda b, h, n: (b, h, 0, 0))],
        scratch_shapes=[pltpu.VMEM((hp, dk, dk), F32), pltpu.VMEM((hp, 5, chunk, dk), F32)],
        compiler_params=_params("parallel", "parallel", "arbitrary"),
        name="hgrn_prompt",
    )(u, u, u, u, lb.reshape(1, width), gain.reshape(1, width))


def _hgrn_sample_kernel(uq_ref, uf_ref, ui_ref, ug_ref, lb_ref, gain_ref, s0_ref, *rest, t):
    o_ref, s_ref = rest[-2:]
    n_prev = s_ref.shape[0] - 1
    for i in range(n_prev):
        s_ref[i] = rest[0][i]
    nb = s0_ref.shape[0]
    width = uq_ref.shape[1]
    dk = HGRN_HEAD_DIM
    heads = width // dk
    pad = dk - 2 * t
    assert pad >= 0
    lb = lb_ref[...]
    row = lax.broadcasted_iota(jnp.int32, (t, 1), 0)
    outs = [[] for _ in range(heads)]
    for sq in range(nb):
        rows = pl.ds(sq * t, t)
        q, k, logf = _hgrn_gates(uq_ref[rows, :], uf_ref[rows, :], lb)
        v = ui_ref[rows, :]
        cum = logf
        sh = 1
        while sh < t:
            cum = cum + jnp.where(row >= sh, pltpu.roll(cum, sh, 0), 0.0)
            sh *= 2
        last = cum[t - 1:t, :]
        qd = q * jnp.exp(cum)
        kd = k * jnp.exp(last - cum)
        dec = jnp.exp(last)
        for h in range(heads):
            cols = slice(h * dk, (h + 1) * dk)
            vh = v[:, cols]
            s0 = s0_ref[sq, h]
            o = jnp.dot(qd[:, cols].astype(MXU_DTYPE), s0.astype(MXU_DTYPE), preferred_element_type=F32)
            for j, rs in _diag_offsets(q[:, cols], k[:, cols], cum[:, cols], row, t):
                o = o + rs * (vh if j == 0 else pltpu.roll(vh, j, 0))
            outs[h].append(o)
            stack_t = jnp.concatenate([kd[:, cols], jnp.broadcast_to(dec[:, cols], (t, dk)),
                                       jnp.zeros((pad, dk), F32)], axis=0).T
            v_pad = jnp.concatenate([vh, jnp.zeros((dk - t, dk), F32)], axis=0)
            s_ref[n_prev, sq, h] = stack_t[:, t:t + 1] * s0 + jnp.dot(stack_t.astype(MXU_DTYPE), v_pad.astype(MXU_DTYPE),
                                                               preferred_element_type=F32)
    for h in range(heads):
        cols = slice(h * dk, (h + 1) * dk)
        o = jnp.concatenate(outs[h], axis=0)
        o_ref[:, cols] = _hgrn_finish(o, ug_ref[:, cols], gain_ref[:, cols], o_ref.dtype)


def _hgrn_sample(u, lb, gain, s0, l, prev, row0, batch, t, width, *, nb=2):
    dk = HGRN_HEAD_DIM
    heads = width // dk
    nb = _tile(batch, nb)
    assert row0 % (nb * t) == 0 and (prev is None) == (l == 0)
    r0 = row0 // (nb * t)
    u_spec = lambda c: pl.BlockSpec((nb * t, width), lambda b: (r0 + b, c))
    vec_spec = pl.BlockSpec((1, width), lambda b: (0, 0))
    stack_spec = lambda n: pl.BlockSpec((n, nb, heads, dk, dk), lambda b: (0, b, 0, 0, 0))
    in_specs = [u_spec(0), u_spec(1), u_spec(2), u_spec(3), vec_spec, vec_spec,
                pl.BlockSpec((None, nb, heads, dk, dk), lambda b: (l, b, 0, 0, 0))]
    args = [u, u, u, u, lb.reshape(1, width), gain.reshape(1, width), s0]
    if prev is not None:
        in_specs.append(stack_spec(l))
        args.append(prev)
    return pl.pallas_call(
        functools.partial(_hgrn_sample_kernel, t=t),
        out_shape=[jax.ShapeDtypeStruct((batch * t, width), MXU_DTYPE),
                   jax.ShapeDtypeStruct((l + 1, batch, heads, dk, dk), F32)],
        grid=(batch // nb,),
        in_specs=in_specs,
        out_specs=[pl.BlockSpec((nb * t, width), lambda b: (b, 0)), stack_spec(l + 1)],
        compiler_params=_params("parallel"),
        name="hgrn_sample",
    )(*args)


def _pool_kernel(p_ref, buf_ref, w_ref, scale_ref, o_ref, nbuf_ref, ext_ref, *, n_past):
    n = pl.program_id(1)
    nb, tb, width = p_ref.shape
    gw = width // len(POOL_WINDOWS)
    hist = POOL_HIST

    @pl.when(n == 0)
    def _():
        ext_ref[:, 0:hist, :] = buf_ref[...]

    @pl.when(n > 0)
    def _():
        ext_ref[:, 0:hist, :] = ext_ref[:, tb:tb + hist, :]

    ext_ref[:, hist:hist + tb, :] = p_ref[...]
    pos = n_past + n * tb + lax.broadcasted_iota(jnp.int32, (1, tb, 1), 1)
    for g, win in enumerate(POOL_WINDOWS):
        cols = slice(g * gw, (g + 1) * gw)
        tot = ext_ref[:, hist:hist + tb, cols]
        x = tot
        for i in range(1, win):
            tot = tot + ext_ref[:, hist - i:hist - i + tb, cols]
        count = jnp.minimum(pos + 1, win).astype(F32)
        pooled = (tot / count - x).reshape(nb * tb, gw)
        mixed = jnp.dot(pooled.astype(MXU_DTYPE), w_ref[g].astype(MXU_DTYPE), preferred_element_type=F32)
        o_ref[:, cols] = (mixed * scale_ref[:, cols]).astype(o_ref.dtype)

    @pl.when(n == pl.num_programs(1) - 1)
    def _():
        nbuf_ref[...] = ext_ref[:, tb:tb + hist, :]


def _pool(u, buf, w_grp, l, scale, col, row0, batch, seq, n_past, *, nb, tb):
    _, groups, gw, _ = w_grp.shape
    width = groups * gw
    hist = POOL_HIST
    nb, tb = _tile(batch, nb), _tile(seq, tb)
    nt = seq // tb
    assert col % width == 0 and row0 % (nb * tb) == 0 and (nb == 1 or nt == 1) and tb % 8 == 0
    if buf is None:
        buf = jnp.zeros((batch, hist, width), F32)
    else:
        buf = jnp.pad(buf, ((0, 0), (1, 0), (0, 0)))
    u3 = u.reshape(u.shape[0] // tb, tb, u.shape[1])
    r0 = row0 // (nb * tb)
    mixed, nbuf = pl.pallas_call(
        functools.partial(_pool_kernel, n_past=n_past),
        out_shape=[jax.ShapeDtypeStruct((batch * seq, width), MXU_DTYPE),
                   jax.ShapeDtypeStruct((batch, hist, width), F32)],
        grid=(batch // nb, nt),
        in_specs=[pl.BlockSpec((nb, tb, width), lambda b, n: (r0 + b * nt + n, 0, col // width)),
                  pl.BlockSpec((nb, hist, width), lambda b, n: (b, 0, 0)),
                  pl.BlockSpec((None, groups, gw, gw), lambda b, n: (l, 0, 0, 0)),
                  pl.BlockSpec((1, width), lambda b, n: (0, 0))],
        out_specs=[pl.BlockSpec((nb * tb, width), lambda b, n: (b * nt + n, 0)),
                   pl.BlockSpec((nb, hist, width), lambda b, n: (b, 0, 0))],
        scratch_shapes=[pltpu.VMEM((nb, hist + tb, width), F32)],
        compiler_params=_params("parallel", "arbitrary"),
        name="pool",
    )(u3, buf, w_grp, scale.reshape(1, width))
    return mixed, nbuf[:, 1:, :]


def _attend_kernel(q_ref, k_ref, v_ref, o_ref, *, tq, hd):
    nb = k_ref.shape[0]
    for b in range(nb):
        rows = slice(b * tq, (b + 1) * tq)
        for h in range(q_ref.shape[1] // hd):
            cols = slice(h * hd, (h + 1) * hd)
            k, v = k_ref[b, :, cols], v_ref[b, :, cols]
            q = q_ref[rows, cols].astype(MXU_DTYPE)
            s = lax.dot_general(q, k.astype(MXU_DTYPE), (((1,), (1,)), ((), ())),
                                preferred_element_type=F32) * (hd ** -0.5)
            e = jnp.exp(s - jnp.max(s, axis=-1, keepdims=True))
            a = e / jnp.sum(e, axis=-1, keepdims=True)
            o_ref[rows, cols] = jnp.dot(a.astype(MXU_DTYPE), v.astype(MXU_DTYPE),
                                        preferred_element_type=F32).astype(o_ref.dtype)


def _attend(u, k_arr, v_arr, k_spec, v_spec, hd, hps, col, row0, batch, seq, *, nb, tq):
    nb, tq = _tile(batch, nb), _tile(seq, tq)
    nt = seq // tq
    wq = hps * hd
    assert col % wq == 0 and row0 % (nb * tq) == 0 and (nb == 1 or nt == 1) and MEM_HEADS % hps == 0
    r0, c0 = row0 // (nb * tq), col // wq
    return pl.pallas_call(
        functools.partial(_attend_kernel, tq=tq, hd=hd),
        out_shape=jax.ShapeDtypeStruct((batch * seq, MEM_HEADS * hd), MXU_DTYPE),
        grid=(batch // nb, nt, MEM_HEADS // hps),
        in_specs=[pl.BlockSpec((nb * tq, wq), lambda b, n, h: (r0 + b * nt + n, c0 + h)), k_spec(nb), v_spec(nb)],
        out_specs=pl.BlockSpec((nb * tq, wq), lambda b, n, h: (b * nt + n, h)),
        compiler_params=_params("parallel", "arbitrary", "arbitrary"),
        name="attend",
    )(u, k_arr, v_arr)


LANES = 128


def _cache_rows(cache):
    depth, batch, tokens, heads, hd = cache.shape
    lt = hd // LANES
    c = cache.reshape(depth, batch, tokens, heads, lt, LANES).transpose(0, 1, 2, 4, 3, 5)
    return c.reshape(depth, batch, tokens * lt * heads, LANES)


def _attend_rows_kernel(q_ref, k_ref, v_ref, o_ref, *, tq, hd):
    nb, n_rows, _ = k_ref.shape
    nlt = hd // LANES
    rper = MEM_HEADS * nlt
    nq = MEM_HEADS * tq
    row = lax.broadcasted_iota(jnp.int32, (nq, n_rows), 0)
    col = lax.broadcasted_iota(jnp.int32, (nq, n_rows), 1)
    valid = (col % rper) == (row // tq)
    for b in range(nb):
        rows = slice(b * tq, (b + 1) * tq)
        qst = jnp.concatenate([q_ref[rows, h * hd + lt * LANES:h * hd + (lt + 1) * LANES]
                               for lt in range(nlt) for h in range(MEM_HEADS)], axis=0).astype(MXU_DTYPE)
        p = lax.dot_general(qst, k_ref[b].astype(MXU_DTYPE), (((1,), (1,)), ((), ())), preferred_element_type=F32)
        s = p[0:nq]
        for lt in range(1, nlt):
            s = s + pltpu.roll(p[lt * nq:(lt + 1) * nq], n_rows - lt * MEM_HEADS, 1)
        s = jnp.where(valid, s * (hd ** -0.5), NEG_BIG)
        e = jnp.exp(s - jnp.max(s, axis=-1, keepdims=True))
        a = e / jnp.sum(e, axis=-1, keepdims=True)
        ast = jnp.concatenate([a if lt == 0 else pltpu.roll(a, lt * MEM_HEADS, 1) for lt in range(nlt)], axis=0)
        o = jnp.dot(ast.astype(MXU_DTYPE), v_ref[b].astype(MXU_DTYPE), preferred_element_type=F32)
        for lt in range(nlt):
            for h in range(MEM_HEADS):
                r0 = lt * nq + h * tq
                o_ref[rows, h * hd + lt * LANES:h * hd + (lt + 1) * LANES] = o[r0:r0 + tq].astype(o_ref.dtype)


def _attend_rows(u, k_rows, v_rows, l, hd, col, row0, batch, tq, *, nb):
    width = MEM_HEADS * hd
    nb = _tile(batch, nb)
    n_rows = k_rows.shape[2]
    assert col % width == 0 and row0 % (nb * tq) == 0 and hd % LANES == 0
    r0, c0 = row0 // (nb * tq), col // width
    kv_spec = pl.BlockSpec((None, nb, n_rows, LANES), lambda b: (l, b, 0, 0))
    return pl.pallas_call(
        functools.partial(_attend_rows_kernel, tq=tq, hd=hd),
        out_shape=jax.ShapeDtypeStruct((batch * tq, width), MXU_DTYPE),
        grid=(batch // nb,),
        in_specs=[pl.BlockSpec((nb * tq, width), lambda b: (r0 + b, c0)), kv_spec, kv_spec],
        out_specs=pl.BlockSpec((nb * tq, width), lambda b: (b, 0)),
        compiler_params=_params("parallel"),
        name="attend_rows",
    )(u, k_rows, v_rows)


PAST_LEN = 16384


def kernel(x_prompt, x_sample, state_hgrn, state_pool, cache_mem_k, cache_mem_v, mem_prompt, norm_pre_mix, norm_post_mix, norm_pre_mlp, norm_post_mlp, norm_mem, w_in, hgrn_lb, hgrn_out_norm, w_pool, pool_scale, w_mem_kv, w_branch_hgrn, w_branch_pool, w_branch_mem, w_out, w_up, w_down):
    bp, sp, d = x_prompt.shape
    bs, ss, _ = x_sample.shape
    depth = w_in.shape[0]
    hw, pw, mw = hgrn_lb.shape[1], pool_scale.shape[1], w_branch_mem.shape[1]
    mp, ms = bp * sp, bs * ss
    tokens = mem_prompt.shape[1]
    heads = hw // HGRN_HEAD_DIM
    pool_col, mem_col, gate_col = 4 * hw, 4 * hw + pw, 4 * hw + pw + mw

    lb_soft = jax.nn.softmax(hgrn_lb.astype(F32), axis=0)
    lb_all = jnp.cumsum(lb_soft, axis=0) - lb_soft[0:1]

    x = jnp.concatenate([x_prompt.reshape(mp, d), x_sample.reshape(ms, d)], axis=0)
    mem2 = mem_prompt.reshape(bp * tokens, d)
    h, h_scale = _rmsnorm(x, norm_pre_mix[0]), None
    hd = mw // MEM_HEADS
    if hd % LANES == 0:
        k_rows, v_rows = _cache_rows(cache_mem_k), _cache_rows(cache_mem_v)
    outs = [[] for _ in range(5)]
    st_s = None
    for l in range(depth):
        kv = _proj(_rmsnorm(mem2, norm_mem[l]), w_mem_kv, l, out_dtype=F32, tm=1024)
        kv3 = kv.reshape(bp, tokens, 2 * mw)
        u, (w_out_l,) = _proj(h, w_in, l, h_scale, out_dtype=F32, casts=[[(w_out, l)]])

        o_p, st_p = _hgrn_prompt(u, lb_all[l], hgrn_out_norm[l], bp, sp, hw)
        o_s, st_s = _hgrn_sample(u, lb_all[l], hgrn_out_norm[l], state_hgrn, l, st_s, mp, bs, ss, hw)
        pool_p, buf_p = _pool(u, None, w_pool, l, pool_scale[l], pool_col, 0, bp, sp, 0, nb=1, tb=512)
        pool_s, buf_s = _pool(u, state_pool[l], w_pool, l, pool_scale[l], pool_col, mp, bs, ss, PAST_LEN,
                              nb=32, tb=ss)
        kv_spec = lambda c: lambda nb: pl.BlockSpec((nb, tokens, mw), lambda b, n, h: (b, 0, c))
        mem_p = _attend(u, kv3, kv3, kv_spec(0), kv_spec(1), hd, MEM_HEADS, mem_col, 0, bp, sp, nb=1, tq=512)
        if hd % LANES == 0:
            mem_s = _attend_rows(u, k_rows, v_rows, l, hd, mem_col, mp, bs, ss, nb=4)
        else:
            mem_s = _attend(u, cache_mem_k[l].reshape(bs, tokens, mw), cache_mem_v[l].reshape(bs, tokens, mw),
                            kv_spec(0), kv_spec(0), hd, MEM_HEADS, mem_col, mp, bs, ss, nb=4, tq=ss)

        merged = _merge((o_p, pool_p, mem_p), (o_s, pool_s, mem_s), w_branch_hgrn, w_branch_pool, w_branch_mem, l,
                        u, gate_col)
        x, h2, h2_scale = _resid(merged, w_out_l, x, norm_post_mix[l], norm_pre_mlp[l])
        ff, (w_down_l,) = _proj(h2, w_up, l, h2_scale, out_dtype=MXU_DTYPE, relu2=True, casts=[[(w_down, l)]])
        x, h, h_scale = _resid(ff, w_down_l, x, norm_post_mlp[l], norm_pre_mix[l + 1] if l + 1 < depth else None)

        for lst, val in zip(outs, (st_p, buf_p, kv3[..., :mw].reshape(bp, tokens, MEM_HEADS, hd),
                                   kv3[..., mw:].reshape(bp, tokens, MEM_HEADS, hd), buf_s)):
            lst.append(val)

    stacked = [jnp.stack(o) for o in outs]
    return (x[:mp].reshape(bp, sp, d), x[mp:].reshape(bs, ss, d), *stacked[:4], st_s, stacked[4])
```

```python
import functools
import math

import jax
import jax.numpy as jnp
from jax import lax
from jax.experimental import pallas as pl
from jax.experimental.pallas import tpu as pltpu

F32 = jnp.float32
MXU_DTYPE = jnp.bfloat16
EPS = 1e-6
F_FLOOR = 1e-30
NEG_BIG = -1e30
LOG2_E = 1.4426950408889634
HGRN_HEAD_DIM = 128
POOL_WINDOWS = (2, 4, 8, 16)
POOL_HIST = 16
MEM_HEADS = 4
N_BRANCHES = 3
V7X_VMEM_LIMIT_BYTES = 56 * 1024 * 1024
HGRN_CHUNK = 128
HGRN_DIAG = 8


def _params(*sem):
    return pltpu.CompilerParams(dimension_semantics=sem, vmem_limit_bytes=V7X_VMEM_LIMIT_BYTES)


def _tile(n, t, quantum=1):
    if n <= t:
        return n
    t -= t % quantum
    while n % t:
        t -= quantum
    assert t > 0, (n, quantum)
    return t


def _rms_scale(x):
    return lax.rsqrt(jnp.mean(x * x, axis=-1, keepdims=True) + EPS)


def _rmsnorm_kernel(x_ref, g_ref, o_ref):
    x = x_ref[...]
    o_ref[...] = (x * _rms_scale(x) * g_ref[...]).astype(o_ref.dtype)


def _rmsnorm(x, g, tm=256):
    m, d = x.shape
    tm = _tile(m, tm)
    return pl.pallas_call(
        _rmsnorm_kernel,
        out_shape=jax.ShapeDtypeStruct((m, d), MXU_DTYPE),
        grid=(m // tm,),
        in_specs=[pl.BlockSpec((tm, d), lambda i: (i, 0)), pl.BlockSpec((1, d), lambda i: (0, 0))],
        out_specs=pl.BlockSpec((tm, d), lambda i: (i, 0)),
        compiler_params=_params("parallel"),
        name="rmsnorm",
    )(x, g.reshape(1, d))


def _proj_kernel(*refs, relu2, has_scale, casts):
    n_in = 2 + has_scale
    srcs = refs[n_in:n_in + len(casts)]
    o_ref = refs[n_in + len(casts)]
    dsts = refs[n_in + len(casts) + 1:]
    t = pl.program_id(0) * pl.num_programs(1) + pl.program_id(1)
    for (start, count), src_ref, dst_ref in zip(casts, srcs, dsts):
        @pl.when(jnp.logical_and(t >= start, t < start + count))
        def _(src_ref=src_ref, dst_ref=dst_ref):
            dst_ref[...] = src_ref[...].astype(dst_ref.dtype)

    a_ref, w_ref = refs[:2]
    acc = jnp.dot(a_ref[...], w_ref[...].astype(MXU_DTYPE), preferred_element_type=F32)
    if has_scale:
        acc = acc * refs[2][...]
    if relu2:
        acc = jnp.square(jnp.maximum(acc, 0.0))
    o_ref[...] = acc.astype(o_ref.dtype)


def _proj(a, w, l, row_scale=None, *, out_dtype, relu2=False, casts=(), tm=2304, tn=512):
    m, k = a.shape
    n = w.shape[-1]
    tm, tn = _tile(m, tm), _tile(n, tn, LANES)
    nj = n // tn
    steps = (m // tm) * nj
    w_spec = (pl.BlockSpec((k, tn), lambda i, j: (0, j)) if w.ndim == 2
              else pl.BlockSpec((None, k, tn), lambda i, j: (l, 0, j)))
    in_specs = [pl.BlockSpec((tm, k), lambda i, j: (i, 0), pipeline_mode=pl.Buffered(1)), w_spec]
    args = [a, w]
    if row_scale is not None:
        in_specs.append(pl.BlockSpec((tm, 1), lambda i, j: (i, 0)))
        args.append(row_scale)
    out_shape = [jax.ShapeDtypeStruct((m, n), out_dtype)]
    out_specs = [pl.BlockSpec((tm, tn), lambda i, j: (i, j))]
    schedule = []
    for group in casts:
        rows = [min(src.shape[1], max(8, 1 << ((1 << 18) // src.shape[2]).bit_length() - 1)) for src, _ in group]
        while sum(-(-src.shape[1] // rw) for (src, _), rw in zip(group, rows)) > steps:
            rows = [min(src.shape[1], 2 * rw) for (src, _), rw in zip(group, rows)]
        start = 0
        for (src, ls), rw in zip(group, rows):
            _, r, c = src.shape
            assert r % rw == 0, (r, rw)
            count = r // rw
            blk = lambda i, j, start=start, count=count: jnp.clip(i * nj + j - start, 0, count - 1)
            in_specs.append(pl.BlockSpec((None, rw, c), lambda i, j, ls=ls, blk=blk: (ls, blk(i, j), 0)))
            args.append(src)
            out_shape.append(jax.ShapeDtypeStruct((r, c), MXU_DTYPE))
            out_specs.append(pl.BlockSpec((rw, c), lambda i, j, blk=blk: (blk(i, j), 0)))
            schedule.append((start, count))
            start += count
    outs = pl.pallas_call(
        functools.partial(_proj_kernel, relu2=relu2, has_scale=row_scale is not None, casts=tuple(schedule)),
        out_shape=out_shape,
        grid=(m // tm, nj),
        in_specs=in_specs,
        out_specs=out_specs,
        compiler_params=_params("arbitrary", "arbitrary"),
        name="proj",
    )(*args)
    return (outs[0], list(outs[1:])) if casts else outs[0]


def _cast_kernel(src_ref, dst_ref):
    dst_ref[...] = src_ref[...].astype(dst_ref.dtype)


def _cast(w, l, rows=256):
    _, r, c = w.shape
    rows = _tile(r, rows)
    return pl.pallas_call(
        _cast_kernel,
        out_shape=jax.ShapeDtypeStruct((r, c), MXU_DTYPE),
        grid=(r // rows,),
        in_specs=[pl.BlockSpec((None, rows, c), lambda i: (l, i, 0))],
        out_specs=pl.BlockSpec((rows, c), lambda i: (i, 0)),
        compiler_params=_params("parallel"),
        name="cast",
    )(w)


def _merge_kernel(*refs, n_first):
    w_refs, g_refs, o_ref = refs[6:9], refs[9:12], refs[12]

    def merged(a_refs):
        tot = None
        for a_ref, w_ref, g_ref in zip(a_refs, w_refs, g_refs):
            y = jnp.dot(a_ref[...], w_ref[...].astype(MXU_DTYPE), preferred_element_type=F32)
            y = jax.nn.sigmoid(g_ref[...]) * y
            tot = y if tot is None else tot + y
        o_ref[...] = tot.astype(o_ref.dtype)

    first = pl.program_id(0) < n_first
    pl.when(first)(lambda: merged(refs[0:3]))
    pl.when(jnp.logical_not(first))(lambda: merged(refs[3:6]))


def _merge(branches_a, branches_b, w_h, w_p, w_m, u, gate_col, *, tm=1024, tn=512):
    ma, mb = branches_a[0].shape[0], branches_b[0].shape[0]
    d = w_h.shape[1]
    tm = _tile(mb, _tile(ma, tm))
    tn = math.gcd(_tile(d, tn, LANES), gate_col)
    gc, nd, na = gate_col // tn, d // tn, ma // tm
    once = pl.Buffered(1)
    a_spec = lambda a: pl.BlockSpec((tm, a.shape[1]), lambda i, j: (jnp.minimum(i, na - 1), 0), pipeline_mode=once)
    b_spec = lambda a: pl.BlockSpec((tm, a.shape[1]), lambda i, j: (jnp.maximum(i - na, 0), 0), pipeline_mode=once)
    w_spec = lambda w: pl.BlockSpec((w.shape[0], tn), lambda i, j: (0, j))
    g_spec = lambda b: pl.BlockSpec((tm, tn), lambda i, j: (i, gc + b * nd + j))
    return pl.pallas_call(
        functools.partial(_merge_kernel, n_first=na),
        out_shape=jax.ShapeDtypeStruct((ma + mb, d), MXU_DTYPE),
        grid=((ma + mb) // tm, nd),
        in_specs=[a_spec(a) for a in branches_a] + [b_spec(a) for a in branches_b]
                 + [w_spec(w_h), w_spec(w_p), w_spec(w_m), g_spec(0), g_spec(1), g_spec(2)],
        out_specs=pl.BlockSpec((tm, tn), lambda i, j: (i, j)),
        compiler_params=_params("parallel", "arbitrary"),
        name="merge",
    )(*branches_a, *branches_b, w_h, w_p, w_m, u, u, u)


def _resid_kernel(a_ref, w_ref, x_ref, gpost_ref, gnext_ref, ox_ref, *rest, nt, nk, nc, tc, want_h):
    if want_h:
        oh_ref, os_ref, acc_ref, s1_ref, s2_ref = rest
    else:
        acc_ref, s1_ref, s2_ref = rest
    i, s = pl.program_id(0), pl.program_id(1)
    cur = i % 2
    prev = 1 - cur
    n_total = nc * tc
    has_prev = i > 0

    @pl.when(jnp.logical_and(has_prev, s == 0))
    def _():
        ss = jnp.zeros(s1_ref.shape, F32)
        for c in range(nc):
            y = acc_ref[prev, c]
            ss = ss + jnp.sum(y * y, axis=-1, keepdims=True)
        s1_ref[...] = lax.rsqrt(ss / n_total + EPS)
        s2_ref[...] = jnp.zeros(s2_ref.shape, F32)

    def epilogue_chunk():
        xn = x_ref[...] + acc_ref[prev, s] * s1_ref[...] * gpost_ref[...]
        ox_ref[...] = xn
        if want_h:
            oh_ref[...] = (xn * gnext_ref[...]).astype(oh_ref.dtype)
            s2_ref[...] += jnp.sum(xn * xn, axis=-1, keepdims=True)

    def k_step(first):
        a = a_ref[...]
        w = w_ref[...].astype(MXU_DTYPE)
        for c in range(nc):
            y = jnp.dot(a, w[:, c * tc:(c + 1) * tc], preferred_element_type=F32)
            if first:
                acc_ref[cur, c] = y
            else:
                acc_ref[cur, c] += y

    do_e = jnp.logical_and(has_prev, s < nc)
    do_k = i < nt
    for e_on in (True, False):
        for first in (True, False, None):
            cond = do_e if e_on else jnp.logical_not(do_e)
            if first is None:
                cond = jnp.logical_and(cond, jnp.logical_not(do_k))
            else:
                cond = jnp.logical_and(jnp.logical_and(cond, do_k), (s == 0) if first else (s > 0))
            if not e_on and first is None:
                continue

            @pl.when(cond)
            def _(e_on=e_on, first=first):
                if e_on:
                    epilogue_chunk()
                if first is not None:
                    k_step(first)

    if want_h:
        @pl.when(jnp.logical_and(has_prev, s == nc - 1))
        def _():
            os_ref[...] = lax.rsqrt(s2_ref[...] / n_total + EPS)


def _resid(a, w, x, g_post, g_next, *, tm=1024, tk=512, tc=512):
    m, k = a.shape
    n = w.shape[1]
    tm, tk, tc = _tile(m, tm), _tile(k, tk, LANES), _tile(n, tc, LANES)
    nt, nk, nc = m // tm, k // tk, n // tc
    assert nk >= nc, (nk, nc)
    want_h = g_next is not None
    if not want_h:
        g_next = g_post
    ti = lambda i: jnp.minimum(i, nt - 1)
    kk = lambda i, s: jnp.where(i < nt, s, nk - 1)
    pi = lambda i: jnp.maximum(i - 1, 0)
    cc = lambda i, s: jnp.where(i > 0, jnp.minimum(s, nc - 1), 0)
    out_shape = [jax.ShapeDtypeStruct((m, n), F32)]
    out_specs = [pl.BlockSpec((tm, tc), lambda i, s: (pi(i), cc(i, s)))]
    if want_h:
        out_shape += [jax.ShapeDtypeStruct((m, n), MXU_DTYPE), jax.ShapeDtypeStruct((m, 1), F32)]
        out_specs += [pl.BlockSpec((tm, tc), lambda i, s: (pi(i), cc(i, s))),
                      pl.BlockSpec((tm, 1), lambda i, s: (pi(i), 0))]
    outs = pl.pallas_call(
        functools.partial(_resid_kernel, nt=nt, nk=nk, nc=nc, tc=tc, want_h=want_h),
        out_shape=out_shape,
        grid=(nt + 1, nk),
        in_specs=[pl.BlockSpec((tm, tk), lambda i, s: (ti(i), kk(i, s))),
                  pl.BlockSpec((tk, n), lambda i, s: (kk(i, s), 0)),
                  pl.BlockSpec((tm, tc), lambda i, s: (pi(i), cc(i, s))),
                  pl.BlockSpec((1, tc), lambda i, s: (0, cc(i, s))),
                  pl.BlockSpec((1, tc), lambda i, s: (0, cc(i, s)))],
        out_specs=out_specs,
        scratch_shapes=[pltpu.VMEM((2, nc, tm, tc), F32), pltpu.VMEM((tm, 1), F32), pltpu.VMEM((tm, 1), F32)],
        compiler_params=_params("arbitrary", "arbitrary"),
        name="resid",
    )(a, w, x, g_post.reshape(1, n), g_next.reshape(1, n))
    return tuple(outs) if want_h else (outs[0], None, None)


def _hgrn_gates(uq, uf, lb):
    f = lb + (1.0 - lb) * jax.nn.sigmoid(uf)
    logf = jnp.log(jnp.maximum(f, F_FLOOR))
    q = jax.nn.silu(uq) * (HGRN_HEAD_DIM ** -0.5)
    return q, 1.0 - f, logf


def _hgrn_finish(o, ug, gain, dtype):
    o = o * _rms_scale(o) * gain
    return (o * jax.nn.silu(ug)).astype(dtype)


def _split3(x):
    h1 = x.astype(MXU_DTYPE)
    r = x - h1.astype(F32)
    h2 = r.astype(MXU_DTYPE)
    h3 = (r - h2.astype(F32)).astype(MXU_DTYPE)
    return h1, h2, h3


def _diag_offsets(q, k, cum, row, block):
    yield 0, jnp.sum(q * k, axis=-1, keepdims=True)
    for j in range(1, block):
        inside = (row % block) >= j
        d = jnp.where(inside, cum - pltpu.roll(cum, j, 0), NEG_BIG)
        yield j, jnp.sum(q * pltpu.roll(k, j, 0) * jnp.exp(d), axis=-1, keepdims=True)


def _diag_blocks(q, k, v, cum, tmp_ref):
    g = HGRN_DIAG
    n = q.shape[0] // g
    for i, a in enumerate((q, k, v, cum)):
        tmp_ref[i] = a
    views = [[tmp_ref[i, pl.ds(r, n, stride=g), :] for r in range(g)] for i in range(4)]
    qs, ks, vs, cs = views
    for r in range(g):
        acc = jnp.sum(qs[r] * ks[r], axis=-1, keepdims=True) * vs[r]
        for r2 in range(r):
            p = qs[r] * ks[r2] * jnp.exp2(cs[r] - cs[r2])
            acc = acc + jnp.sum(p, axis=-1, keepdims=True) * vs[r2]
        tmp_ref[4, pl.ds(r, n, stride=g), :] = acc
    return tmp_ref[4]


def _hgrn_chunk(q, k, v, logf, st, tmp_ref):
    c = q.shape[0]
    row = lax.broadcasted_iota(jnp.int32, (c, 1), 0)
    rr = lax.broadcasted_iota(jnp.int32, (c, c), 0)
    cc = lax.broadcasted_iota(jnp.int32, (c, c), 1)
    tri = (cc <= rr).astype(MXU_DTYPE)
    cum = sum(jnp.dot(tri, h, preferred_element_type=F32) for h in _split3(logf)[::-1]) * LOG2_E

    o_diag = _diag_blocks(q, k, v, cum, tmp_ref)
    scores = jnp.zeros((c, c), F32)
    b = HGRN_DIAG
    while b < c:
        c3 = cum.reshape(c // (2 * b), 2 * b, cum.shape[1])
        e = jnp.exp2(-jnp.abs(c3 - c3[:, b - 1:b, :])).reshape(cum.shape)
        odd = ((row // b) % 2) == 1
        qt = jnp.where(odd, q * e, 0.0).astype(MXU_DTYPE)
        kt = jnp.where(odd, 0.0, k * e).astype(MXU_DTYPE)
        s = lax.dot_general(qt, kt, (((1,), (1,)), ((), ())), preferred_element_type=F32)
        scores = scores + jnp.where((rr // (2 * b)) == (cc // (2 * b)), s, 0.0)
        b *= 2

    vm = v.astype(MXU_DTYPE)
    qd = (q * jnp.exp2(cum)).astype(MXU_DTYPE)
    o = lax.dot_general(qd, st.astype(MXU_DTYPE), (((1,), (1,)), ((), ())), preferred_element_type=F32)
    o = o + jnp.dot(scores.astype(MXU_DTYPE), vm, preferred_element_type=F32) + o_diag
    last = cum[c - 1:c, :]
    kd = (k * jnp.exp2(last - cum)).astype(MXU_DTYPE)
    st_new = st * jnp.exp2(last) + jnp.dot(v.T.astype(MXU_DTYPE), kd, preferred_element_type=F32)
    return o, st_new


def _hgrn_prompt_kernel(uq_ref, uf_ref, ui_ref, ug_ref, lb_ref, gain_ref, o_ref, s_ref, st_ref, tmp_ref, *, chunk):
    n = pl.program_id(2)

    @pl.when(n == 0)
    def _():
        st_ref[...] = jnp.zeros(st_ref.shape, F32)

    dk = HGRN_HEAD_DIM
    for ci in range(uq_ref.shape[0] // chunk):
        rows = pl.ds(ci * chunk, chunk)
        for hh in range(uq_ref.shape[1] // dk):
            cols = slice(hh * dk, (hh + 1) * dk)
            q, k, logf = _hgrn_gates(uq_ref[rows, cols], uf_ref[rows, cols], lb_ref[:, cols])
            o, st_new = _hgrn_chunk(q, k, ui_ref[rows, cols], logf, st_ref[hh], tmp_ref.at[hh])
            st_ref[hh] = st_new
            o_ref[rows, cols] = _hgrn_finish(o, ug_ref[rows, cols], gain_ref[:, cols], o_ref.dtype)

    @pl.when(n == pl.num_programs(2) - 1)
    def _():
        for hh in range(st_ref.shape[0]):
            s_ref[hh] = st_ref[hh].T


def _hgrn_prompt(u, lb, gain, batch, seq, width, *, tb=512, hp=4):
    dk = HGRN_HEAD_DIM
    heads = width // dk
    chunk = min(HGRN_CHUNK, seq)
    tb, hp = _tile(seq, tb), _tile(heads, hp)
    assert tb % chunk == 0
    nt, hg = seq // tb, heads // hp
    u_spec = lambda c: pl.BlockSpec((tb, hp * dk), lambda b, h, n: (b * nt + n, c * hg + h))
    vec_spec = pl.BlockSpec((1, hp * dk), lambda b, h, n: (0, h))
    return pl.pallas_call(
        functools.partial(_hgrn_prompt_kernel, chunk=chunk),
        out_shape=[jax.ShapeDtypeStruct((batch * seq, width), MXU_DTYPE),
                   jax.ShapeDtypeStruct((batch, heads, dk, dk), F32)],
        grid=(batch, hg, nt),
        in_specs=[u_spec(0), u_spec(1), u_spec(2), u_spec(3), vec_spec, vec_spec],
        out_specs=[pl.BlockSpec((tb, hp * dk), lambda b, h, n: (b * nt + n, h)),
                   pl.BlockSpec((None, hp, dk, dk), lambda b, h, n: (b, h, 0, 0))],
        scratch_shapes=[pltpu.VMEM((hp, dk, dk), F32), pltpu.VMEM((hp, 5, chunk, dk), F32)],
        compiler_params=_params("parallel", "parallel", "arbitrary"),
        name="hgrn_prompt",
    )(u, u, u, u, lb.reshape(1, width), gain.reshape(1, width))


def _hgrn_sample_kernel(uq_ref, uf_ref, ui_ref, ug_ref, lb_ref, gain_ref, s0_ref, *rest, t):
    o_ref, s_ref = rest[-2:]
    n_prev = s_ref.shape[0] - 1
    for i in range(n_prev):
        s_ref[i] = rest[0][i]
    nb = s0_ref.shape[0]
    width = uq_ref.shape[1]
    dk = HGRN_HEAD_DIM
    heads = width // dk
    pad = dk - 2 * t
    assert pad >= 0
    lb = lb_ref[...]
    row = lax.broadcasted_iota(jnp.int32, (t, 1), 0)
    outs = [[] for _ in range(heads)]
    for sq in range(nb):
        rows = pl.ds(sq * t, t)
        q, k, logf = _hgrn_gates(uq_ref[rows, :], uf_ref[rows, :], lb)
        v = ui_ref[rows, :]
        cum = logf
        sh = 1
        while sh < t:
            cum = cum + jnp.where(row >= sh, pltpu.roll(cum, sh, 0), 0.0)
            sh *= 2
        last = cum[t - 1:t, :]
        qd = q * jnp.exp(cum)
        kd = k * jnp.exp(last - cum)
        dec = jnp.exp(last)
        for h in range(heads):
            cols = slice(h * dk, (h + 1) * dk)
            vh = v[:, cols]
            s0 = s0_ref[sq, h]
            o = jnp.dot(qd[:, cols].astype(MXU_DTYPE), s0.astype(MXU_DTYPE), preferred_element_type=F32)
            for j, rs in _diag_offsets(q[:, cols], k[:, cols], cum[:, cols], row, t):
                o = o + rs * (vh if j == 0 else pltpu.roll(vh, j, 0))
            outs[h].append(o)
            stack_t = jnp.concatenate([kd[:, cols], jnp.broadcast_to(dec[:, cols], (t, dk)),
                                       jnp.zeros((pad, dk), F32)], axis=0).T
            v_pad = jnp.concatenate([vh, jnp.zeros((dk - t, dk), F32)], axis=0)
            s_ref[n_prev, sq, h] = stack_t[:, t:t + 1] * s0 + jnp.dot(stack_t.astype(MXU_DTYPE), v_pad.astype(MXU_DTYPE),
                                                               preferred_element_type=F32)
    for h in range(heads):
        cols = slice(h * dk, (h + 1) * dk)
        o = jnp.concatenate(outs[h], axis=0)
        o_ref[:, cols] = _hgrn_finish(o, ug_ref[:, cols], gain_ref[:, cols], o_ref.dtype)


def _hgrn_sample(u, lb, gain, s0, l, prev, row0, batch, t, width, *, nb=2):
    dk = HGRN_HEAD_DIM
    heads = width // dk
    nb = _tile(batch, nb)
    assert row0 % (nb * t) == 0 and (prev is None) == (l == 0)
    r0 = row0 // (nb * t)
    u_spec = lambda c: pl.BlockSpec((nb * t, width), lambda b: (r0 + b, c))
    vec_spec = pl.BlockSpec((1, width), lambda b: (0, 0))
    stack_spec = lambda n: pl.BlockSpec((n, nb, heads, dk, dk), lambda b: (0, b, 0, 0, 0))
    in_specs = [u_spec(0), u_spec(1), u_spec(2), u_spec(3), vec_spec, vec_spec,
                pl.BlockSpec((None, nb, heads, dk, dk), lambda b: (l, b, 0, 0, 0))]
    args = [u, u, u, u, lb.reshape(1, width), gain.reshape(1, width), s0]
    if prev is not None:
        in_specs.append(stack_spec(l))
        args.append(prev)
    return pl.pallas_call(
        functools.partial(_hgrn_sample_kernel, t=t),
        out_shape=[jax.ShapeDtypeStruct((batch * t, width), MXU_DTYPE),
                   jax.ShapeDtypeStruct((l + 1, batch, heads, dk, dk), F32)],
        grid=(batch // nb,),
        in_specs=in_specs,
        out_specs=[pl.BlockSpec((nb * t, width), lambda b: (b, 0)), stack_spec(l + 1)],
        compiler_params=_params("parallel"),
        name="hgrn_sample",
    )(*args)


def _pool_kernel(p_ref, buf_ref, w_ref, scale_ref, o_ref, nbuf_ref, ext_ref, *, n_past):
    n = pl.program_id(1)
    nb, tb, width = p_ref.shape
    gw = width // len(POOL_WINDOWS)
    hist = POOL_HIST

    @pl.when(n == 0)
    def _():
        ext_ref[:, 0:hist, :] = buf_ref[...]

    @pl.when(n > 0)
    def _():
        ext_ref[:, 0:hist, :] = ext_ref[:, tb:tb + hist, :]

    ext_ref[:, hist:hist + tb, :] = p_ref[...]
    pos = n_past + n * tb + lax.broadcasted_iota(jnp.int32, (1, tb, 1), 1)
    for g, win in enumerate(POOL_WINDOWS):
        cols = slice(g * gw, (g + 1) * gw)
        tot = ext_ref[:, hist:hist + tb, cols]
        x = tot
        for i in range(1, win):
            tot = tot + ext_ref[:, hist - i:hist - i + tb, cols]
        count = jnp.minimum(pos + 1, win).astype(F32)
        pooled = (tot / count - x).reshape(nb * tb, gw)
        mixed = jnp.dot(pooled.astype(MXU_DTYPE), w_ref[g].astype(MXU_DTYPE), preferred_element_type=F32)
        o_ref[:, cols] = (mixed * scale_ref[:, cols]).astype(o_ref.dtype)

    @pl.when(n == pl.num_programs(1) - 1)
    def _():
        nbuf_ref[...] = ext_ref[:, tb:tb + hist, :]


def _pool(u, buf, w_grp, l, scale, col, row0, batch, seq, n_past, *, nb, tb):
    _, groups, gw, _ = w_grp.shape
    width = groups * gw
    hist = POOL_HIST
    nb, tb = _tile(batch, nb), _tile(seq, tb)
    nt = seq // tb
    assert col % width == 0 and row0 % (nb * tb) == 0 and (nb == 1 or nt == 1) and tb % 8 == 0
    if buf is None:
        buf = jnp.zeros((batch, hist, width), F32)
    else:
        buf = jnp.pad(buf, ((0, 0), (1, 0), (0, 0)))
    u3 = u.reshape(u.shape[0] // tb, tb, u.shape[1])
    r0 = row0 // (nb * tb)
    mixed, nbuf = pl.pallas_call(
        functools.partial(_pool_kernel, n_past=n_past),
        out_shape=[jax.ShapeDtypeStruct((batch * seq, width), MXU_DTYPE),
                   jax.ShapeDtypeStruct((batch, hist, width), F32)],
        grid=(batch // nb, nt),
        in_specs=[pl.BlockSpec((nb, tb, width), lambda b, n: (r0 + b * nt + n, 0, col // width)),
                  pl.BlockSpec((nb, hist, width), lambda b, n: (b, 0, 0)),
                  pl.BlockSpec((None, groups, gw, gw), lambda b, n: (l, 0, 0, 0)),
                  pl.BlockSpec((1, width), lambda b, n: (0, 0))],
        out_specs=[pl.BlockSpec((nb * tb, width), lambda b, n: (b * nt + n, 0)),
                   pl.BlockSpec((nb, hist, width), lambda b, n: (b, 0, 0))],
        scratch_shapes=[pltpu.VMEM((nb, hist + tb, width), F32)],
        compiler_params=_params("parallel", "arbitrary"),
        name="pool",
    )(u3, buf, w_grp, scale.reshape(1, width))
    return mixed, nbuf[:, 1:, :]


def _attend_kernel(q_ref, k_ref, v_ref, o_ref, *, tq, hd):
    nb = k_ref.shape[0]
    for b in range(nb):
        rows = slice(b * tq, (b + 1) * tq)
        for h in range(q_ref.shape[1] // hd):
            cols = slice(h * hd, (h + 1) * hd)
            k, v = k_ref[b, :, cols], v_ref[b, :, cols]
            q = q_ref[rows, cols].astype(MXU_DTYPE)
            s = lax.dot_general(q, k.astype(MXU_DTYPE), (((1,), (1,)), ((), ())),
                                preferred_element_type=F32) * (hd ** -0.5)
            e = jnp.exp(s - jnp.max(s, axis=-1, keepdims=True))
            a = e / jnp.sum(e, axis=-1, keepdims=True)
            o_ref[rows, cols] = jnp.dot(a.astype(MXU_DTYPE), v.astype(MXU_DTYPE),
                                        preferred_element_type=F32).astype(o_ref.dtype)


def _attend(u, k_arr, v_arr, k_spec, v_spec, hd, hps, col, row0, batch, seq, *, nb, tq):
    nb, tq = _tile(batch, nb), _tile(seq, tq)
    nt = seq // tq
    wq = hps * hd
    assert col % wq == 0 and row0 % (nb * tq) == 0 and (nb == 1 or nt == 1) and MEM_HEADS % hps == 0
    r0, c0 = row0 // (nb * tq), col // wq
    return pl.pallas_call(
        functools.partial(_attend_kernel, tq=tq, hd=hd),
        out_shape=jax.ShapeDtypeStruct((batch * seq, MEM_HEADS * hd), MXU_DTYPE),
        grid=(batch // nb, nt, MEM_HEADS // hps),
        in_specs=[pl.BlockSpec((nb * tq, wq), lambda b, n, h: (r0 + b * nt + n, c0 + h)), k_spec(nb), v_spec(nb)],
        out_specs=pl.BlockSpec((nb * tq, wq), lambda b, n, h: (b * nt + n, h)),
        compiler_params=_params("parallel", "arbitrary", "arbitrary"),
        name="attend",
    )(u, k_arr, v_arr)


LANES = 128


def _cache_rows(cache):
    depth, batch, tokens, heads, hd = cache.shape
    lt = hd // LANES
    c = cache.reshape(depth, batch, tokens, heads, lt, LANES).transpose(0, 1, 2, 4, 3, 5)
    return c.reshape(depth, batch, tokens * lt * heads, LANES)


def _attend_rows_kernel(q_ref, k_ref, v_ref, o_ref, *, tq, hd):
    nb, n_rows, _ = k_ref.shape
    nlt = hd // LANES
    rper = MEM_HEADS * nlt
    nq = MEM_HEADS * tq
    row = lax.broadcasted_iota(jnp.int32, (nq, n_rows), 0)
    col = lax.broadcasted_iota(jnp.int32, (nq, n_rows), 1)
    valid = (col % rper) == (row // tq)
    for b in range(nb):
        rows = slice(b * tq, (b + 1) * tq)
        qst = jnp.concatenate([q_ref[rows, h * hd + lt * LANES:h * hd + (lt + 1) * LANES]
                               for lt in range(nlt) for h in range(MEM_HEADS)], axis=0).astype(MXU_DTYPE)
        p = lax.dot_general(qst, k_ref[b].astype(MXU_DTYPE), (((1,), (1,)), ((), ())), preferred_element_type=F32)
        s = p[0:nq]
        for lt in range(1, nlt):
            s = s + pltpu.roll(p[lt * nq:(lt + 1) * nq], n_rows - lt * MEM_HEADS, 1)
        s = jnp.where(valid, s * (hd ** -0.5), NEG_BIG)
        e = jnp.exp(s - jnp.max(s, axis=-1, keepdims=True))
        a = e / jnp.sum(e, axis=-1, keepdims=True)
        ast = jnp.concatenate([a if lt == 0 else pltpu.roll(a, lt * MEM_HEADS, 1) for lt in range(nlt)], axis=0)
        o = jnp.dot(ast.astype(MXU_DTYPE), v_ref[b].astype(MXU_DTYPE), preferred_element_type=F32)
        for lt in range(nlt):
            for h in range(MEM_HEADS):
                r0 = lt * nq + h * tq
                o_ref[rows, h * hd + lt * LANES:h * hd + (lt + 1) * LANES] = o[r0:r0 + tq].astype(o_ref.dtype)


def _attend_rows(u, k_rows, v_rows, l, hd, col, row0, batch, tq, *, nb):
    width = MEM_HEADS * hd
    nb = _tile(batch, nb)
    n_rows = k_rows.shape[2]
    assert col % width == 0 and row0 % (nb * tq) == 0 and hd % LANES == 0
    r0, c0 = row0 // (nb * tq), col // width
    kv_spec = pl.BlockSpec((None, nb, n_rows, LANES), lambda b: (l, b, 0, 0))
    return pl.pallas_call(
        functools.partial(_attend_rows_kernel, tq=tq, hd=hd),
        out_shape=jax.ShapeDtypeStruct((batch * tq, width), MXU_DTYPE),
        grid=(batch // nb,),
        in_specs=[pl.BlockSpec((nb * tq, width), lambda b: (r0 + b, c0)), kv_spec, kv_spec],
        out_specs=pl.BlockSpec((nb * tq, width), lambda b: (b, 0)),
        compiler_params=_params("parallel"),
        name="attend_rows",
    )(u, k_rows, v_rows)


PAST_LEN = 16384


def kernel(x_prompt, x_sample, state_hgrn, state_pool, cache_mem_k, cache_mem_v, mem_prompt, norm_pre_mix, norm_post_mix, norm_pre_mlp, norm_post_mlp, norm_mem, w_in, hgrn_lb, hgrn_out_norm, w_pool, pool_scale, w_mem_kv, w_branch_hgrn, w_branch_pool, w_branch_mem, w_out, w_up, w_down):
    bp, sp, d = x_prompt.shape
    bs, ss, _ = x_sample.shape
    depth = w_in.shape[0]
    hw, pw, mw = hgrn_lb.shape[1], pool_scale.shape[1], w_branch_mem.shape[1]
    mp, ms = bp * sp, bs * ss
    tokens = mem_prompt.shape[1]
    heads = hw // HGRN_HEAD_DIM
    pool_col, mem_col, gate_col = 4 * hw, 4 * hw + pw, 4 * hw + pw + mw

    lb_soft = jax.nn.softmax(hgrn_lb.astype(F32), axis=0)
    lb_all = jnp.cumsum(lb_soft, axis=0) - lb_soft[0:1]

    x = jnp.concatenate([x_prompt.reshape(mp, d), x_sample.reshape(ms, d)], axis=0)
    mem2 = mem_prompt.reshape(bp * tokens, d)
    h, h_scale = _rmsnorm(x, norm_pre_mix[0]), None
    hd = mw // MEM_HEADS
    if hd % LANES == 0:
        k_rows, v_rows = _cache_rows(cache_mem_k), _cache_rows(cache_mem_v)
    outs = [[] for _ in range(5)]
    st_s = None
    for l in range(depth):
        kv = _proj(_rmsnorm(mem2, norm_mem[l]), w_mem_kv, l, out_dtype=F32, tm=1024)
        kv3 = kv.reshape(bp, tokens, 2 * mw)
        u, (w_out_l,) = _proj(h, w_in, l, h_scale, out_dtype=F32, casts=[[(w_out, l)]])

        o_p, st_p = _hgrn_prompt(u, lb_all[l], hgrn_out_norm[l], bp, sp, hw)
        o_s, st_s = _hgrn_sample(u, lb_all[l], hgrn_out_norm[l], state_hgrn, l, st_s, mp, bs, ss, hw)
        pool_p, buf_p = _pool(u, None, w_pool, l, pool_scale[l], pool_col, 0, bp, sp, 0, nb=1, tb=512)
        pool_s, buf_s = _pool(u, state_pool[l], w_pool, l, pool_scale[l], pool_col, mp, bs, ss, PAST_LEN,
                              nb=32, tb=ss)
        kv_spec = lambda c: lambda nb: pl.BlockSpec((nb, tokens, mw), lambda b, n, h: (b, 0, c))
        mem_p = _attend(u, kv3, kv3, kv_spec(0), kv_spec(1), hd, MEM_HEADS, mem_col, 0, bp, sp, nb=1, tq=512)
        if hd % LANES == 0:
            mem_s = _attend_rows(u, k_rows, v_rows, l, hd, mem_col, mp, bs, ss, nb=4)
        else:
            mem_s = _attend(u, cache_mem_k[l].reshape(bs, tokens, mw), cache_mem_v[l].reshape(bs, tokens, mw),
                            kv_spec(0), kv_spec(0), hd, MEM_HEADS, mem_col, mp, bs, ss, nb=4, tq=ss)

        merged = _merge((o_p, pool_p, mem_p), (o_s, pool_s, mem_s), _cast(w_branch_hgrn, l), _cast(w_branch_pool, l),
                        _cast(w_branch_mem, l), u, gate_col)
        x, h2, h2_scale = _resid(merged, w_out_l, x, norm_post_mix[l], norm_pre_mlp[l])
        ff, (w_down_l,) = _proj(h2, w_up, l, h2_scale, out_dtype=MXU_DTYPE, relu2=True, casts=[[(w_down, l)]])
        x, h, h_scale = _resid(ff, w_down_l, x, norm_post_mlp[l], norm_pre_mix[l + 1] if l + 1 < depth else None)

        for lst, val in zip(outs, (st_p, buf_p, kv3[..., :mw].reshape(bp, tokens, MEM_HEADS, hd),
                                   kv3[..., mw:].reshape(bp, tokens, MEM_HEADS, hd), buf_s)):
            lst.append(val)

    stacked = [jnp.stack(o) for o in outs]
    return (x[:mp].reshape(bp, sp, d), x[mp:].reshape(bs, ss, d), *stacked[:4], st_s, stacked[4])
```

```python
import functools
import math

import jax
import jax.numpy as jnp
from jax import lax
from jax.experimental import pallas as pl
from jax.experimental.pallas import tpu as pltpu

F32 = jnp.float32
MXU_DTYPE = jnp.bfloat16
EPS = 1e-6
F_FLOOR = 1e-30
NEG_BIG = -1e30
LOG2_E = 1.4426950408889634
HGRN_HEAD_DIM = 128
POOL_WINDOWS = (2, 4, 8, 16)
POOL_HIST = 16
MEM_HEADS = 4
N_BRANCHES = 3
PAST_LEN = 16384
LANES = 128
V7X_VMEM_LIMIT_BYTES = 56 * 1024 * 1024
HGRN_CHUNK = 128
HGRN_DIAG = 8
CAST_STEP_ELEMS = 1 << 18


def _params(*sem):
    return pltpu.CompilerParams(dimension_semantics=sem, vmem_limit_bytes=V7X_VMEM_LIMIT_BYTES)


def _tile(n, t, quantum=1):
    if n <= t:
        return n
    t -= t % quantum
    while n % t:
        t -= quantum
    assert t > 0, (n, quantum)
    return t


def _rms_scale(x):
    return lax.rsqrt(jnp.mean(x * x, axis=-1, keepdims=True) + EPS)


def _rmsnorm_kernel(x_ref, g_ref, o_ref):
    x = x_ref[...]
    o_ref[...] = (x * _rms_scale(x) * g_ref[...]).astype(o_ref.dtype)


def _rmsnorm(x, g, tm=256):
    m, d = x.shape
    tm = _tile(m, tm)
    return pl.pallas_call(
        _rmsnorm_kernel,
        out_shape=jax.ShapeDtypeStruct((m, d), MXU_DTYPE),
        grid=(m // tm,),
        in_specs=[pl.BlockSpec((tm, d), lambda i: (i, 0)), pl.BlockSpec((1, d), lambda i: (0, 0))],
        out_specs=pl.BlockSpec((tm, d), lambda i: (i, 0)),
        compiler_params=_params("parallel"),
        name="rmsnorm",
    )(x, g.reshape(1, d))


def _proj_kernel(*refs, relu2, has_scale, n_cast):
    if n_cast:
        src_ref, o_ref, dst_ref = refs[-3:]

        @pl.when(pl.program_id(0) * pl.num_programs(1) + pl.program_id(1) < n_cast)
        def _():
            dst_ref[...] = src_ref[...].astype(dst_ref.dtype)
    else:
        o_ref = refs[-1]
    a_ref, w_ref = refs[:2]
    acc = jnp.dot(a_ref[...], w_ref[...].astype(MXU_DTYPE), preferred_element_type=F32)
    if has_scale:
        acc = acc * refs[2][...]
    if relu2:
        acc = jnp.square(jnp.maximum(acc, 0.0))
    o_ref[...] = acc.astype(o_ref.dtype)


def _proj(a, w, l, row_scale=None, *, out_dtype, relu2=False, cast_src=None, tm=2304, tn=512):
    m, k = a.shape
    n = w.shape[2]
    tm, tn = _tile(m, tm), _tile(n, tn, LANES)
    nj = n // tn
    steps = (m // tm) * nj
    in_specs = [pl.BlockSpec((tm, k), lambda i, j: (i, 0), pipeline_mode=pl.Buffered(1)),
                pl.BlockSpec((None, k, tn), lambda i, j: (l, 0, j))]
    args = [a, w]
    if row_scale is not None:
        in_specs.append(pl.BlockSpec((tm, 1), lambda i, j: (i, 0)))
        args.append(row_scale)
    out_shape = [jax.ShapeDtypeStruct((m, n), out_dtype)]
    out_specs = [pl.BlockSpec((tm, tn), lambda i, j: (i, j))]
    n_cast = 0
    if cast_src is not None:
        _, r, c = cast_src.shape
        rows = min(r, max(8, CAST_STEP_ELEMS // c))
        while r % rows or r // rows > steps:
            rows += 1
        n_cast = r // rows
        blk = lambda i, j: jnp.minimum(i * nj + j, n_cast - 1)
        in_specs.append(pl.BlockSpec((None, rows, c), lambda i, j: (l, blk(i, j), 0)))
        args.append(cast_src)
        out_shape.append(jax.ShapeDtypeStruct((r, c), MXU_DTYPE))
        out_specs.append(pl.BlockSpec((rows, c), lambda i, j: (blk(i, j), 0)))
    outs = pl.pallas_call(
        functools.partial(_proj_kernel, relu2=relu2, has_scale=row_scale is not None, n_cast=n_cast),
        out_shape=out_shape,
        grid=(m // tm, nj),
        in_specs=in_specs,
        out_specs=out_specs,
        compiler_params=_params("arbitrary", "arbitrary"),
        name="proj",
    )(*args)
    return tuple(outs) if n_cast else outs[0]


def _merge_kernel(*refs, n_first):
    w_refs, g_refs, o_ref = refs[6:9], refs[9:12], refs[12]

    def merged(a_refs):
        tot = None
        for a_ref, w_ref, g_ref in zip(a_refs, w_refs, g_refs):
            y = jnp.dot(a_ref[...], w_ref[...].astype(MXU_DTYPE), preferred_element_type=F32)
            y = jax.nn.sigmoid(g_ref[...]) * y
            tot = y if tot is None else tot + y
        o_ref[...] = tot.astype(o_ref.dtype)

    first = pl.program_id(0) < n_first
    pl.when(first)(lambda: merged(refs[0:3]))
    pl.when(jnp.logical_not(first))(lambda: merged(refs[3:6]))


def _merge(branches_a, branches_b, w_h, w_p, w_m, l, u, gate_col, *, tm=1024, tn=512):
    ma, mb = branches_a[0].shape[0], branches_b[0].shape[0]
    d = w_h.shape[2]
    tm = _tile(mb, _tile(ma, tm))
    tn = math.gcd(_tile(d, tn, LANES), gate_col)
    gc, nd, na = gate_col // tn, d // tn, ma // tm
    once = pl.Buffered(1)
    a_spec = lambda a: pl.BlockSpec((tm, a.shape[1]), lambda i, j: (jnp.minimum(i, na - 1), 0), pipeline_mode=once)
    b_spec = lambda a: pl.BlockSpec((tm, a.shape[1]), lambda i, j: (jnp.maximum(i - na, 0), 0), pipeline_mode=once)
    w_spec = lambda w: pl.BlockSpec((None, w.shape[1], tn), lambda i, j: (l, 0, j))
    g_spec = lambda b: pl.BlockSpec((tm, tn), lambda i, j: (i, gc + b * nd + j))
    return pl.pallas_call(
        functools.partial(_merge_kernel, n_first=na),
        out_shape=jax.ShapeDtypeStruct((ma + mb, d), MXU_DTYPE),
        grid=((ma + mb) // tm, nd),
        in_specs=[a_spec(a) for a in branches_a] + [b_spec(a) for a in branches_b]
                 + [w_spec(w_h), w_spec(w_p), w_spec(w_m), g_spec(0), g_spec(1), g_spec(2)],
        out_specs=pl.BlockSpec((tm, tn), lambda i, j: (i, j)),
        compiler_params=_params("parallel", "arbitrary"),
        name="merge",
    )(*branches_a, *branches_b, w_h, w_p, w_m, u, u, u)


def _resid_kernel(a_ref, w_ref, x_ref, gpost_ref, gnext_ref, ox_ref, *rest, nt, nk, nc, tc, want_h):
    if want_h:
        oh_ref, os_ref, acc_ref, s1_ref, s2_ref = rest
    else:
        acc_ref, s1_ref, s2_ref = rest
    i, s = pl.program_id(0), pl.program_id(1)
    cur = i % 2
    prev = 1 - cur
    n_total = nc * tc
    has_prev = i > 0

    @pl.when(jnp.logical_and(has_prev, s == 0))
    def _():
        ss = jnp.zeros(s1_ref.shape, F32)
        for c in range(nc):
            y = acc_ref[prev, c]
            ss = ss + jnp.sum(y * y, axis=-1, keepdims=True)
        s1_ref[...] = lax.rsqrt(ss / n_total + EPS)
        s2_ref[...] = jnp.zeros(s2_ref.shape, F32)

    def epilogue_chunk():
        xn = x_ref[...] + acc_ref[prev, s] * s1_ref[...] * gpost_ref[...]
        ox_ref[...] = xn
        if want_h:
            oh_ref[...] = (xn * gnext_ref[...]).astype(oh_ref.dtype)
            s2_ref[...] += jnp.sum(xn * xn, axis=-1, keepdims=True)

    def k_step(first):
        a = a_ref[...]
        w = w_ref[...].astype(MXU_DTYPE)
        for c in range(nc):
            y = jnp.dot(a, w[:, c * tc:(c + 1) * tc], preferred_element_type=F32)
            if first:
                acc_ref[cur, c] = y
            else:
                acc_ref[cur, c] += y

    do_e = jnp.logical_and(has_prev, s < nc)
    do_k = i < nt
    for e_on in (True, False):
        for first in (True, False, None):
            cond = do_e if e_on else jnp.logical_not(do_e)
            if first is None:
                cond = jnp.logical_and(cond, jnp.logical_not(do_k))
            else:
                cond = jnp.logical_and(jnp.logical_and(cond, do_k), (s == 0) if first else (s > 0))
            if not e_on and first is None:
                continue

            @pl.when(cond)
            def _(e_on=e_on, first=first):
                if e_on:
                    epilogue_chunk()
                if first is not None:
                    k_step(first)

    if want_h:
        @pl.when(jnp.logical_and(has_prev, s == nc - 1))
        def _():
            os_ref[...] = lax.rsqrt(s2_ref[...] / n_total + EPS)


def _resid(a, w, x, g_post, g_next, *, tm=1024, tk=512, tc=512):
    m, k = a.shape
    n = w.shape[1]
    tm, tk, tc = _tile(m, tm), _tile(k, tk, LANES), _tile(n, tc, LANES)
    nt, nk, nc = m // tm, k // tk, n // tc
    assert nk >= nc, (nk, nc)
    want_h = g_next is not None
    if not want_h:
        g_next = g_post
    ti = lambda i: jnp.minimum(i, nt - 1)
    kk = lambda i, s: jnp.where(i < nt, s, nk - 1)
    pi = lambda i: jnp.maximum(i - 1, 0)
    cc = lambda i, s: jnp.where(i > 0, jnp.minimum(s, nc - 1), 0)
    out_shape = [jax.ShapeDtypeStruct((m, n), F32)]
    out_specs = [pl.BlockSpec((tm, tc), lambda i, s: (pi(i), cc(i, s)))]
    if want_h:
        out_shape += [jax.ShapeDtypeStruct((m, n), MXU_DTYPE), jax.ShapeDtypeStruct((m, 1), F32)]
        out_specs += [pl.BlockSpec((tm, tc), lambda i, s: (pi(i), cc(i, s))),
                      pl.BlockSpec((tm, 1), lambda i, s: (pi(i), 0))]
    outs = pl.pallas_call(
        functools.partial(_resid_kernel, nt=nt, nk=nk, nc=nc, tc=tc, want_h=want_h),
        out_shape=out_shape,
        grid=(nt + 1, nk),
        in_specs=[pl.BlockSpec((tm, tk), lambda i, s: (ti(i), kk(i, s))),
                  pl.BlockSpec((tk, n), lambda i, s: (kk(i, s), 0)),
                  pl.BlockSpec((tm, tc), lambda i, s: (pi(i), cc(i, s))),
                  pl.BlockSpec((1, tc), lambda i, s: (0, cc(i, s))),
                  pl.BlockSpec((1, tc), lambda i, s: (0, cc(i, s)))],
        out_specs=out_specs,
        scratch_shapes=[pltpu.VMEM((2, nc, tm, tc), F32), pltpu.VMEM((tm, 1), F32), pltpu.VMEM((tm, 1), F32)],
        compiler_params=_params("arbitrary", "arbitrary"),
        name="resid",
    )(a, w, x, g_post.reshape(1, n), g_next.reshape(1, n))
    return tuple(outs) if want_h else (outs[0], None, None)


def _hgrn_gates(uq, uf, lb):
    f = lb + (1.0 - lb) * jax.nn.sigmoid(uf)
    logf = jnp.log(jnp.maximum(f, F_FLOOR))
    q = jax.nn.silu(uq) * (HGRN_HEAD_DIM ** -0.5)
    return q, 1.0 - f, logf


def _hgrn_finish(o, ug, gain, dtype):
    o = o * _rms_scale(o) * gain
    return (o * jax.nn.silu(ug)).astype(dtype)


def _split3(x):
    h1 = x.astype(MXU_DTYPE)
    r = x - h1.astype(F32)
    h2 = r.astype(MXU_DTYPE)
    h3 = (r - h2.astype(F32)).astype(MXU_DTYPE)
    return h1, h2, h3


def _diag_offsets(q, k, cum, row, block):
    yield 0, jnp.sum(q * k, axis=-1, keepdims=True)
    for j in range(1, block):
        inside = (row % block) >= j
        d = jnp.where(inside, cum - pltpu.roll(cum, j, 0), NEG_BIG)
        yield j, jnp.sum(q * pltpu.roll(k, j, 0) * jnp.exp(d), axis=-1, keepdims=True)


def _diag_blocks(q, k, v, cum, tmp_ref):
    g = HGRN_DIAG
    n = q.shape[0] // g
    for i, a in enumerate((q, k, v, cum)):
        tmp_ref[i] = a
    views = [[tmp_ref[i, pl.ds(r, n, stride=g), :] for r in range(g)] for i in range(4)]
    qs, ks, vs, cs = views
    for r in range(g):
        acc = jnp.sum(qs[r] * ks[r], axis=-1, keepdims=True) * vs[r]
        for r2 in range(r):
            p = qs[r] * ks[r2] * jnp.exp2(cs[r] - cs[r2])
            acc = acc + jnp.sum(p, axis=-1, keepdims=True) * vs[r2]
        tmp_ref[4, pl.ds(r, n, stride=g), :] = acc
    return tmp_ref[4]


def _hgrn_chunk(q, k, v, logf, st, tmp_ref):
    c = q.shape[0]
    row = lax.broadcasted_iota(jnp.int32, (c, 1), 0)
    rr = lax.broadcasted_iota(jnp.int32, (c, c), 0)
    cc = lax.broadcasted_iota(jnp.int32, (c, c), 1)
    tri = (cc <= rr).astype(MXU_DTYPE)
    cum = sum(jnp.dot(tri, h, preferred_element_type=F32) for h in _split3(logf)[::-1]) * LOG2_E

    o_diag = _diag_blocks(q, k, v, cum, tmp_ref)
    scores = jnp.zeros((c, c), F32)
    b = HGRN_DIAG
    while b < c:
        c3 = cum.reshape(c // (2 * b), 2 * b, cum.shape[1])
        e = jnp.exp2(-jnp.abs(c3 - c3[:, b - 1:b, :])).reshape(cum.shape)
        odd = ((row // b) % 2) == 1
        qt = jnp.where(odd, q * e, 0.0).astype(MXU_DTYPE)
        kt = jnp.where(odd, 0.0, k * e).astype(MXU_DTYPE)
        s = lax.dot_general(qt, kt, (((1,), (1,)), ((), ())), preferred_element_type=F32)
        scores = scores + jnp.where((rr // (2 * b)) == (cc // (2 * b)), s, 0.0)
        b *= 2

    vm = v.astype(MXU_DTYPE)
    qd = (q * jnp.exp2(cum)).astype(MXU_DTYPE)
    o = lax.dot_general(qd, st.astype(MXU_DTYPE), (((1,), (1,)), ((), ())), preferred_element_type=F32)
    o = o + jnp.dot(scores.astype(MXU_DTYPE), vm, preferred_element_type=F32) + o_diag
    last = cum[c - 1:c, :]
    kd = (k * jnp.exp2(last - cum)).astype(MXU_DTYPE)
    st_new = st * jnp.exp2(last) + jnp.dot(v.T.astype(MXU_DTYPE), kd, preferred_element_type=F32)
    return o, st_new


def _hgrn_prompt_kernel(uq_ref, uf_ref, ui_ref, ug_ref, lb_ref, gain_ref, o_ref, s_ref, st_ref, tmp_ref, *, chunk):
    n = pl.program_id(2)

    @pl.when(n == 0)
    def _():
        st_ref[...] = jnp.zeros(st_ref.shape, F32)

    dk = HGRN_HEAD_DIM
    for ci in range(uq_ref.shape[0] // chunk):
        rows = pl.ds(ci * chunk, chunk)
        for hh in range(uq_ref.shape[1] // dk):
            cols = slice(hh * dk, (hh + 1) * dk)
            q, k, logf = _hgrn_gates(uq_ref[rows, cols], uf_ref[rows, cols], lb_ref[:, cols])
            o, st_new = _hgrn_chunk(q, k, ui_ref[rows, cols], logf, st_ref[hh], tmp_ref.at[hh])
            st_ref[hh] = st_new
            o_ref[rows, cols] = _hgrn_finish(o, ug_ref[rows, cols], gain_ref[:, cols], o_ref.dtype)

    @pl.when(n == pl.num_programs(2) - 1)
    def _():
        for hh in range(st_ref.shape[0]):
            s_ref[hh] = st_ref[hh].T


def _hgrn_prompt(u, lb, gain, batch, seq, width, *, tb=512, hp=4):
    dk = HGRN_HEAD_DIM
    heads = width // dk
    chunk = min(HGRN_CHUNK, seq)
    tb, hp = _tile(seq, tb), _tile(heads, hp)
    assert tb % chunk == 0
    nt, hg = seq // tb, heads // hp
    u_spec = lambda c: pl.BlockSpec((tb, hp * dk), lambda b, h, n: (b * nt + n, c * hg + h))
    vec_spec = pl.BlockSpec((1, hp * dk), lambda b, h, n: (0, h))
    return pl.pallas_call(
        functools.partial(_hgrn_prompt_kernel, chunk=chunk),
        out_shape=[jax.ShapeDtypeStruct((batch * seq, width), MXU_DTYPE),
                   jax.ShapeDtypeStruct((batch, heads, dk, dk), F32)],
        grid=(batch, hg, nt),
        in_specs=[u_spec(0), u_spec(1), u_spec(2), u_spec(3), vec_spec, vec_spec],
        out_specs=[pl.BlockSpec((tb, hp * dk), lambda b, h, n: (b * nt + n, h)),
                   pl.BlockSpec((None, hp, dk, dk), lambda b, h, n: (b, h, 0, 0))],
        scratch_shapes=[pltpu.VMEM((hp, dk, dk), F32), pltpu.VMEM((hp, 5, chunk, dk), F32)],
        compiler_params=_params("parallel", "parallel", "arbitrary"),
        name="hgrn_prompt",
    )(u, u, u, u, lb.reshape(1, width), gain.reshape(1, width))


def _hgrn_sample_kernel(uq_ref, uf_ref, ui_ref, ug_ref, lb_ref, gain_ref, s0_ref, *rest, t):
    o_ref, s_ref = rest[-2:]
    n_prev = s_ref.shape[0] - 1
    for i in range(n_prev):
        s_ref[i] = rest[0][i]
    nb = s0_ref.shape[0]
    width = uq_ref.shape[1]
    dk = HGRN_HEAD_DIM
    heads = width // dk
    pad = dk - 2 * t
    assert pad >= 0
    lb = lb_ref[...]
    row = lax.broadcasted_iota(jnp.int32, (t, 1), 0)
    outs = [[] for _ in range(heads)]
    for sq in range(nb):
        rows = pl.ds(sq * t, t)
        q, k, logf = _hgrn_gates(uq_ref[rows, :], uf_ref[rows, :], lb)
        v = ui_ref[rows, :]
        cum = logf
        sh = 1
        while sh < t:
            cum = cum + jnp.where(row >= sh, pltpu.roll(cum, sh, 0), 0.0)
            sh *= 2
        last = cum[t - 1:t, :]
        qd = q * jnp.exp(cum)
        kd = k * jnp.exp(last - cum)
        dec = jnp.exp(last)
        for h in range(heads):
            cols = slice(h * dk, (h + 1) * dk)
            vh = v[:, cols]
            s0 = s0_ref[sq, h]
            o = jnp.dot(qd[:, cols].astype(MXU_DTYPE), s0.astype(MXU_DTYPE), preferred_element_type=F32)
            for j, rs in _diag_offsets(q[:, cols], k[:, cols], cum[:, cols], row, t):
                o = o + rs * (vh if j == 0 else pltpu.roll(vh, j, 0))
            outs[h].append(o)
            stack_t = jnp.concatenate([kd[:, cols], jnp.broadcast_to(dec[:, cols], (t, dk)),
                                       jnp.zeros((pad, dk), F32)], axis=0).T
            v_pad = jnp.concatenate([vh, jnp.zeros((dk - t, dk), F32)], axis=0)
            s_ref[n_prev, sq, h] = stack_t[:, t:t + 1] * s0 + jnp.dot(stack_t.astype(MXU_DTYPE), v_pad.astype(MXU_DTYPE),
                                                               preferred_element_type=F32)
    for h in range(heads):
        cols = slice(h * dk, (h + 1) * dk)
        o = jnp.concatenate(outs[h], axis=0)
        o_ref[:, cols] = _hgrn_finish(o, ug_ref[:, cols], gain_ref[:, cols], o_ref.dtype)


def _hgrn_sample(u, lb, gain, s0, l, prev, row0, batch, t, width, *, nb=2):
    dk = HGRN_HEAD_DIM
    heads = width // dk
    nb = _tile(batch, nb)
    assert row0 % (nb * t) == 0 and (prev is None) == (l == 0)
    r0 = row0 // (nb * t)
    u_spec = lambda c: pl.BlockSpec((nb * t, width), lambda b: (r0 + b, c))
    vec_spec = pl.BlockSpec((1, width), lambda b: (0, 0))
    stack_spec = lambda n: pl.BlockSpec((n, nb, heads, dk, dk), lambda b: (0, b, 0, 0, 0))
    in_specs = [u_spec(0), u_spec(1), u_spec(2), u_spec(3), vec_spec, vec_spec,
                pl.BlockSpec((None, nb, heads, dk, dk), lambda b: (l, b, 0, 0, 0))]
    args = [u, u, u, u, lb.reshape(1, width), gain.reshape(1, width), s0]
    if prev is not None:
        in_specs.append(stack_spec(l))
        args.append(prev)
    return pl.pallas_call(
        functools.partial(_hgrn_sample_kernel, t=t),
        out_shape=[jax.ShapeDtypeStruct((batch * t, width), MXU_DTYPE),
                   jax.ShapeDtypeStruct((l + 1, batch, heads, dk, dk), F32)],
        grid=(batch // nb,),
        in_specs=in_specs,
        out_specs=[pl.BlockSpec((nb * t, width), lambda b: (b, 0)), stack_spec(l + 1)],
        compiler_params=_params("parallel"),
        name="hgrn_sample",
    )(*args)


def _pool_kernel(p_ref, buf_ref, w_ref, scale_ref, o_ref, nbuf_ref, ext_ref, *, n_past):
    n = pl.program_id(1)
    nb, tb, width = p_ref.shape
    gw = width // len(POOL_WINDOWS)
    hist = POOL_HIST

    @pl.when(n == 0)
    def _():
        ext_ref[:, 0:hist, :] = buf_ref[...]

    @pl.when(n > 0)
    def _():
        ext_ref[:, 0:hist, :] = ext_ref[:, tb:tb + hist, :]

    ext_ref[:, hist:hist + tb, :] = p_ref[...]
    pos = n_past + n * tb + lax.broadcasted_iota(jnp.int32, (1, tb, 1), 1)
    for g, win in enumerate(POOL_WINDOWS):
        cols = slice(g * gw, (g + 1) * gw)
        tot = ext_ref[:, hist:hist + tb, cols]
        x = tot
        for i in range(1, win):
            tot = tot + ext_ref[:, hist - i:hist - i + tb, cols]
        count = jnp.minimum(pos + 1, win).astype(F32)
        pooled = (tot / count - x).reshape(nb * tb, gw)
        mixed = jnp.dot(pooled.astype(MXU_DTYPE), w_ref[g].astype(MXU_DTYPE), preferred_element_type=F32)
        o_ref[:, cols] = (mixed * scale_ref[:, cols]).astype(o_ref.dtype)

    @pl.when(n == pl.num_programs(1) - 1)
    def _():
        nbuf_ref[...] = ext_ref[:, tb:tb + hist, :]


def _pool(u, buf, w_grp, l, scale, col, row0, batch, seq, n_past, *, nb, tb):
    _, groups, gw, _ = w_grp.shape
    width = groups * gw
    hist = POOL_HIST
    nb, tb = _tile(batch, nb), _tile(seq, tb)
    nt = seq // tb
    assert col % width == 0 and row0 % (nb * tb) == 0 and (nb == 1 or nt == 1) and tb % 8 == 0
    if buf is None:
        buf = jnp.zeros((batch, hist, width), F32)
    else:
        buf = jnp.pad(buf, ((0, 0), (1, 0), (0, 0)))
    u3 = u.reshape(u.shape[0] // tb, tb, u.shape[1])
    r0 = row0 // (nb * tb)
    mixed, nbuf = pl.pallas_call(
        functools.partial(_pool_kernel, n_past=n_past),
        out_shape=[jax.ShapeDtypeStruct((batch * seq, width), MXU_DTYPE),
                   jax.ShapeDtypeStruct((batch, hist, width), F32)],
        grid=(batch // nb, nt),
        in_specs=[pl.BlockSpec((nb, tb, width), lambda b, n: (r0 + b * nt + n, 0, col // width)),
                  pl.BlockSpec((nb, hist, width), lambda b, n: (b, 0, 0)),
                  pl.BlockSpec((None, groups, gw, gw), lambda b, n: (l, 0, 0, 0)),
                  pl.BlockSpec((1, width), lambda b, n: (0, 0))],
        out_specs=[pl.BlockSpec((nb * tb, width), lambda b, n: (b * nt + n, 0)),
                   pl.BlockSpec((nb, hist, width), lambda b, n: (b, 0, 0))],
        scratch_shapes=[pltpu.VMEM((nb, hist + tb, width), F32)],
        compiler_params=_params("parallel", "arbitrary"),
        name="pool",
    )(u3, buf, w_grp, scale.reshape(1, width))
    return mixed, nbuf[:, 1:, :]


def _attend_kernel(q_ref, k_ref, v_ref, o_ref, *, tq, hd):
    nb = k_ref.shape[0]
    for b in range(nb):
        rows = slice(b * tq, (b + 1) * tq)
        for h in range(q_ref.shape[1] // hd):
            cols = slice(h * hd, (h + 1) * hd)
            k, v = k_ref[b, :, cols], v_ref[b, :, cols]
            q = q_ref[rows, cols].astype(MXU_DTYPE)
            s = lax.dot_general(q, k.astype(MXU_DTYPE), (((1,), (1,)), ((), ())),
                                preferred_element_type=F32) * (hd ** -0.5)
            e = jnp.exp(s - jnp.max(s, axis=-1, keepdims=True))
            a = e / jnp.sum(e, axis=-1, keepdims=True)
            o_ref[rows, cols] = jnp.dot(a.astype(MXU_DTYPE), v.astype(MXU_DTYPE),
                                        preferred_element_type=F32).astype(o_ref.dtype)


def _attend(u, k_arr, v_arr, k_spec, v_spec, hd, hps, col, row0, batch, seq, *, nb, tq):
    nb, tq = _tile(batch, nb), _tile(seq, tq)
    nt = seq // tq
    wq = hps * hd
    assert col % wq == 0 and row0 % (nb * tq) == 0 and (nb == 1 or nt == 1) and MEM_HEADS % hps == 0
    r0, c0 = row0 // (nb * tq), col // wq
    return pl.pallas_call(
        functools.partial(_attend_kernel, tq=tq, hd=hd),
        out_shape=jax.ShapeDtypeStruct((batch * seq, MEM_HEADS * hd), MXU_DTYPE),
        grid=(batch // nb, nt, MEM_HEADS // hps),
        in_specs=[pl.BlockSpec((nb * tq, wq), lambda b, n, h: (r0 + b * nt + n, c0 + h)), k_spec(nb), v_spec(nb)],
        out_specs=pl.BlockSpec((nb * tq, wq), lambda b, n, h: (b * nt + n, h)),
        compiler_params=_params("parallel", "arbitrary", "arbitrary"),
        name="attend",
    )(u, k_arr, v_arr)


def _cache_rows(cache):
    depth, batch, tokens, heads, hd = cache.shape
    lt = hd // LANES
    c = cache.reshape(depth, batch, tokens, heads, lt, LANES).transpose(0, 1, 2, 4, 3, 5)
    return c.reshape(depth, batch, tokens * lt * heads, LANES)


def _attend_rows_kernel(q_ref, k_ref, v_ref, o_ref, *, tq, hd):
    nb, n_rows, _ = k_ref.shape
    nlt = hd // LANES
    rper = MEM_HEADS * nlt
    nq = MEM_HEADS * tq
    row = lax.broadcasted_iota(jnp.int32, (nq, n_rows), 0)
    col = lax.broadcasted_iota(jnp.int32, (nq, n_rows), 1)
    valid = (col % rper) == (row // tq)
    for b in range(nb):
        rows = slice(b * tq, (b + 1) * tq)
        qst = jnp.concatenate([q_ref[rows, h * hd + lt * LANES:h * hd + (lt + 1) * LANES]
                               for lt in range(nlt) for h in range(MEM_HEADS)], axis=0).astype(MXU_DTYPE)
        p = lax.dot_general(qst, k_ref[b].astype(MXU_DTYPE), (((1,), (1,)), ((), ())), preferred_element_type=F32)
        s = p[0:nq]
        for lt in range(1, nlt):
            s = s + pltpu.roll(p[lt * nq:(lt + 1) * nq], n_rows - lt * MEM_HEADS, 1)
        s = jnp.where(valid, s * (hd ** -0.5), NEG_BIG)
        e = jnp.exp(s - jnp.max(s, axis=-1, keepdims=True))
        a = e / jnp.sum(e, axis=-1, keepdims=True)
        ast = jnp.concatenate([a if lt == 0 else pltpu.roll(a, lt * MEM_HEADS, 1) for lt in range(nlt)], axis=0)
        o = jnp.dot(ast.astype(MXU_DTYPE), v_ref[b].astype(MXU_DTYPE), preferred_element_type=F32)
        for lt in range(nlt):
            for h in range(MEM_HEADS):
                r0 = lt * nq + h * tq
                o_ref[rows, h * hd + lt * LANES:h * hd + (lt + 1) * LANES] = o[r0:r0 + tq].astype(o_ref.dtype)


def _attend_rows(u, k_rows, v_rows, l, hd, col, row0, batch, tq, *, nb):
    width = MEM_HEADS * hd
    nb = _tile(batch, nb)
    n_rows = k_rows.shape[2]
    assert col % width == 0 and row0 % (nb * tq) == 0 and hd % LANES == 0
    r0, c0 = row0 // (nb * tq), col // width
    kv_spec = pl.BlockSpec((None, nb, n_rows, LANES), lambda b: (l, b, 0, 0))
    return pl.pallas_call(
        functools.partial(_attend_rows_kernel, tq=tq, hd=hd),
        out_shape=jax.ShapeDtypeStruct((batch * tq, width), MXU_DTYPE),
        grid=(batch // nb,),
        in_specs=[pl.BlockSpec((nb * tq, width), lambda b: (r0 + b, c0)), kv_spec, kv_spec],
        out_specs=pl.BlockSpec((nb * tq, width), lambda b: (b, 0)),
        compiler_params=_params("parallel"),
        name="attend_rows",
    )(u, k_rows, v_rows)


def kernel(x_prompt, x_sample, state_hgrn, state_pool, cache_mem_k, cache_mem_v, mem_prompt, norm_pre_mix, norm_post_mix, norm_pre_mlp, norm_post_mlp, norm_mem, w_in, hgrn_lb, hgrn_out_norm, w_pool, pool_scale, w_mem_kv, w_branch_hgrn, w_branch_pool, w_branch_mem, w_out, w_up, w_down):
    bp, sp, d = x_prompt.shape
    bs, ss, _ = x_sample.shape
    depth = w_in.shape[0]
    hw, pw, mw = hgrn_lb.shape[1], pool_scale.shape[1], w_branch_mem.shape[1]
    mp, ms = bp * sp, bs * ss
    tokens = mem_prompt.shape[1]
    heads = hw // HGRN_HEAD_DIM
    pool_col, mem_col, gate_col = 4 * hw, 4 * hw + pw, 4 * hw + pw + mw

    lb_soft = jax.nn.softmax(hgrn_lb.astype(F32), axis=0)
    lb_all = jnp.cumsum(lb_soft, axis=0) - lb_soft[0:1]

    x = jnp.concatenate([x_prompt.reshape(mp, d), x_sample.reshape(ms, d)], axis=0)
    mem2 = mem_prompt.reshape(bp * tokens, d)
    h, h_scale = _rmsnorm(x, norm_pre_mix[0]), None
    hd = mw // MEM_HEADS
    if hd % LANES == 0:
        k_rows, v_rows = _cache_rows(cache_mem_k), _cache_rows(cache_mem_v)
    outs = [[] for _ in range(5)]
    st_s = None
    for l in range(depth):
        kv = _proj(_rmsnorm(mem2, norm_mem[l]), w_mem_kv, l, out_dtype=F32, tm=1024)
        kv3 = kv.reshape(bp, tokens, 2 * mw)
        u, w_out_l = _proj(h, w_in, l, h_scale, out_dtype=F32, cast_src=w_out)

        o_p, st_p = _hgrn_prompt(u, lb_all[l], hgrn_out_norm[l], bp, sp, hw)
        o_s, st_s = _hgrn_sample(u, lb_all[l], hgrn_out_norm[l], state_hgrn, l, st_s, mp, bs, ss, hw)
        pool_p, buf_p = _pool(u, None, w_pool, l, pool_scale[l], pool_col, 0, bp, sp, 0, nb=1, tb=512)
        pool_s, buf_s = _pool(u, state_pool[l], w_pool, l, pool_scale[l], pool_col, mp, bs, ss, PAST_LEN,
                              nb=32, tb=ss)
        kv_spec = lambda c: lambda nb: pl.BlockSpec((nb, tokens, mw), lambda b, n, h: (b, 0, c))
        mem_p = _attend(u, kv3, kv3, kv_spec(0), kv_spec(1), hd, MEM_HEADS, mem_col, 0, bp, sp, nb=1, tq=512)
        if hd % LANES == 0:
            mem_s = _attend_rows(u, k_rows, v_rows, l, hd, mem_col, mp, bs, ss, nb=4)
        else:
            mem_s = _attend(u, cache_mem_k[l].reshape(bs, tokens, mw), cache_mem_v[l].reshape(bs, tokens, mw),
                            kv_spec(0), kv_spec(0), hd, MEM_HEADS, mem_col, mp, bs, ss, nb=4, tq=ss)

        merged = _merge((o_p, pool_p, mem_p), (o_s, pool_s, mem_s), w_branch_hgrn, w_branch_pool, w_branch_mem, l,
                        u, gate_col)
        x, h2, h2_scale = _resid(merged, w_out_l, x, norm_post_mix[l], norm_pre_mlp[l])
        ff, w_down_l = _proj(h2, w_up, l, h2_scale, out_dtype=MXU_DTYPE, relu2=True, cast_src=w_down)
        x, h, h_scale = _resid(ff, w_down_l, x, norm_post_mlp[l], norm_pre_mix[l + 1] if l + 1 < depth else None)

        for lst, val in zip(outs, (st_p, buf_p, kv3[..., :mw].reshape(bp, tokens, MEM_HEADS, hd),
                                   kv3[..., mw:].reshape(bp, tokens, MEM_HEADS, hd), buf_s)):
            lst.append(val)

    stacked = [jnp.stack(o) for o in outs]
    return (x[:mp].reshape(bp, sp, d), x[mp:].reshape(bs, ss, d), *stacked[:4], st_s, stacked[4])
```

```python
import functools
import math

import jax
import jax.numpy as jnp
from jax import lax
from jax.experimental import pallas as pl
from jax.experimental.pallas import tpu as pltpu

F32 = jnp.float32
MXU_DTYPE = jnp.bfloat16
EPS = 1e-6
F_FLOOR = 1e-30
NEG_BIG = -1e30
LOG2_E = 1.4426950408889634
HGRN_HEAD_DIM = 128
POOL_WINDOWS = (2, 4, 8, 16)
POOL_HIST = 16
MEM_HEADS = 4
N_BRANCHES = 3
PAST_LEN = 16384
LANES = 128
V7X_VMEM_LIMIT_BYTES = 56 * 1024 * 1024
HGRN_CHUNK = 128
HGRN_DIAG = 8
CAST_STEP_ELEMS = 1 << 18


def _params(*sem):
    return pltpu.CompilerParams(dimension_semantics=sem, vmem_limit_bytes=V7X_VMEM_LIMIT_BYTES)


def _tile(n, t, quantum=1):
    if n <= t:
        return n
    t -= t % quantum
    while n % t:
        t -= quantum
    assert t > 0, (n, quantum)
    return t


def _rms_scale(x):
    return lax.rsqrt(jnp.mean(x * x, axis=-1, keepdims=True) + EPS)


def _rmsnorm_kernel(x_ref, g_ref, o_ref):
    x = x_ref[...]
    o_ref[...] = (x * _rms_scale(x) * g_ref[...]).astype(o_ref.dtype)


def _rmsnorm(x, g, tm=256):
    m, d = x.shape
    tm = _tile(m, tm)
    return pl.pallas_call(
        _rmsnorm_kernel,
        out_shape=jax.ShapeDtypeStruct((m, d), MXU_DTYPE),
        grid=(m // tm,),
        in_specs=[pl.BlockSpec((tm, d), lambda i: (i, 0)), pl.BlockSpec((1, d), lambda i: (0, 0))],
        out_specs=pl.BlockSpec((tm, d), lambda i: (i, 0)),
        compiler_params=_params("parallel"),
        name="rmsnorm",
    )(x, g.reshape(1, d))


def _rmsnorm_stack_kernel(xa_ref, xb_ref, g_ref, ox_ref, oh_ref, *, n_first):
    def emit(x_ref):
        x = x_ref[...]
        ox_ref[...] = x
        oh_ref[...] = (x * _rms_scale(x) * g_ref[...]).astype(oh_ref.dtype)

    first = pl.program_id(0) < n_first
    pl.when(first)(lambda: emit(xa_ref))
    pl.when(jnp.logical_not(first))(lambda: emit(xb_ref))


def _rmsnorm_stack(xa, xb, g, tm=256):
    (ma, d), mb = xa.shape, xb.shape[0]
    tm = _tile(mb, _tile(ma, tm))
    na = ma // tm
    row_spec = pl.BlockSpec((tm, d), lambda i: (i, 0))
    return pl.pallas_call(
        functools.partial(_rmsnorm_stack_kernel, n_first=na),
        out_shape=[jax.ShapeDtypeStruct((ma + mb, d), F32), jax.ShapeDtypeStruct((ma + mb, d), MXU_DTYPE)],
        grid=((ma + mb) // tm,),
        in_specs=[pl.BlockSpec((tm, d), lambda i: (jnp.minimum(i, na - 1), 0)),
                  pl.BlockSpec((tm, d), lambda i: (jnp.maximum(i - na, 0), 0)),
                  pl.BlockSpec((1, d), lambda i: (0, 0))],
        out_specs=[row_spec, row_spec],
        compiler_params=_params("arbitrary"),
        name="rmsnorm_stack",
    )(xa, xb, g.reshape(1, d))


def _proj_kernel(*refs, relu2, has_scale, n_cast):
    if n_cast:
        src_ref, o_ref, dst_ref = refs[-3:]

        @pl.when(pl.program_id(0) * pl.num_programs(1) + pl.program_id(1) < n_cast)
        def _():
            dst_ref[...] = src_ref[...].astype(dst_ref.dtype)
    else:
        o_ref = refs[-1]
    a_ref, w_ref = refs[:2]
    acc = jnp.dot(a_ref[...], w_ref[...].astype(MXU_DTYPE), preferred_element_type=F32)
    if has_scale:
        acc = acc * refs[2][...]
    if relu2:
        acc = jnp.square(jnp.maximum(acc, 0.0))
    o_ref[...] = acc.astype(o_ref.dtype)


def _proj(a, w, l, row_scale=None, *, out_dtype, relu2=False, cast_src=None, tm=2304, tn=512):
    m, k = a.shape
    n = w.shape[2]
    tm, tn = _tile(m, tm), _tile(n, tn, LANES)
    nj = n // tn
    steps = (m // tm) * nj
    in_specs = [pl.BlockSpec((tm, k), lambda i, j: (i, 0), pipeline_mode=pl.Buffered(1)),
                pl.BlockSpec((None, k, tn), lambda i, j: (l, 0, j))]
    args = [a, w]
    if row_scale is not None:
        in_specs.append(pl.BlockSpec((tm, 1), lambda i, j: (i, 0)))
        args.append(row_scale)
    out_shape = [jax.ShapeDtypeStruct((m, n), out_dtype)]
    out_specs = [pl.BlockSpec((tm, tn), lambda i, j: (i, j))]
    n_cast = 0
    if cast_src is not None:
        _, r, c = cast_src.shape
        rows = min(r, max(8, CAST_STEP_ELEMS // c))
        while r % rows or r // rows > steps:
            rows += 1
        n_cast = r // rows
        blk = lambda i, j: jnp.minimum(i * nj + j, n_cast - 1)
        in_specs.append(pl.BlockSpec((None, rows, c), lambda i, j: (l, blk(i, j), 0)))
        args.append(cast_src)
        out_shape.append(jax.ShapeDtypeStruct((r, c), MXU_DTYPE))
        out_specs.append(pl.BlockSpec((rows, c), lambda i, j: (blk(i, j), 0)))
    outs = pl.pallas_call(
        functools.partial(_proj_kernel, relu2=relu2, has_scale=row_scale is not None, n_cast=n_cast),
        out_shape=out_shape,
        grid=(m // tm, nj),
        in_specs=in_specs,
        out_specs=out_specs,
        compiler_params=_params("arbitrary", "arbitrary"),
        name="proj",
    )(*args)
    return tuple(outs) if n_cast else outs[0]


def _merge_kernel(*refs, n_first):
    w_refs, g_refs, o_ref = refs[6:9], refs[9:12], refs[12]

    def merged(a_refs):
        tot = None
        for a_ref, w_ref, g_ref in zip(a_refs, w_refs, g_refs):
            y = jnp.dot(a_ref[...], w_ref[...].astype(MXU_DTYPE), preferred_element_type=F32)
            y = jax.nn.sigmoid(g_ref[...]) * y
            tot = y if tot is None else tot + y
        o_ref[...] = tot.astype(o_ref.dtype)

    first = pl.program_id(0) < n_first
    pl.when(first)(lambda: merged(refs[0:3]))
    pl.when(jnp.logical_not(first))(lambda: merged(refs[3:6]))


def _merge(branches_a, branches_b, w_h, w_p, w_m, l, u, gate_col, *, tm=1024, tn=512):
    ma, mb = branches_a[0].shape[0], branches_b[0].shape[0]
    d = w_h.shape[2]
    tm = _tile(mb, _tile(ma, tm))
    tn = math.gcd(_tile(d, tn, LANES), gate_col)
    gc, nd, na = gate_col // tn, d // tn, ma // tm
    once = pl.Buffered(1)
    a_spec = lambda a: pl.BlockSpec((tm, a.shape[1]), lambda i, j: (jnp.minimum(i, na - 1), 0), pipeline_mode=once)
    b_spec = lambda a: pl.BlockSpec((tm, a.shape[1]), lambda i, j: (jnp.maximum(i - na, 0), 0), pipeline_mode=once)
    w_spec = lambda w: pl.BlockSpec((None, w.shape[1], tn), lambda i, j: (l, 0, j))
    g_spec = lambda b: pl.BlockSpec((tm, tn), lambda i, j: (i, gc + b * nd + j))
    return pl.pallas_call(
        functools.partial(_merge_kernel, n_first=na),
        out_shape=jax.ShapeDtypeStruct((ma + mb, d), MXU_DTYPE),
        grid=((ma + mb) // tm, nd),
        in_specs=[a_spec(a) for a in branches_a] + [b_spec(a) for a in branches_b]
                 + [w_spec(w_h), w_spec(w_p), w_spec(w_m), g_spec(0), g_spec(1), g_spec(2)],
        out_specs=pl.BlockSpec((tm, tn), lambda i, j: (i, j)),
        compiler_params=_params("parallel", "arbitrary"),
        name="merge",
    )(*branches_a, *branches_b, w_h, w_p, w_m, u, u, u)


def _resid_kernel(a_ref, w_ref, x_ref, gpost_ref, gnext_ref, ox_ref, *rest, nt, nk, nc, tc, want_h):
    if want_h:
        oh_ref, os_ref, acc_ref, s1_ref, s2_ref = rest
    else:
        acc_ref, s1_ref, s2_ref = rest
    i, s = pl.program_id(0), pl.program_id(1)
    cur = i % 2
    prev = 1 - cur
    n_total = nc * tc
    has_prev = i > 0

    @pl.when(jnp.logical_and(has_prev, s == 0))
    def _():
        ss = jnp.zeros(s1_ref.shape, F32)
        for c in range(nc):
            y = acc_ref[prev, c]
            ss = ss + jnp.sum(y * y, axis=-1, keepdims=True)
        s1_ref[...] = lax.rsqrt(ss / n_total + EPS)
        s2_ref[...] = jnp.zeros(s2_ref.shape, F32)

    def epilogue_chunk():
        xn = x_ref[...] + acc_ref[prev, s] * s1_ref[...] * gpost_ref[...]
        ox_ref[...] = xn
        if want_h:
            oh_ref[...] = (xn * gnext_ref[...]).astype(oh_ref.dtype)
            s2_ref[...] += jnp.sum(xn * xn, axis=-1, keepdims=True)

    def k_step(first):
        a = a_ref[...]
        w = w_ref[...].astype(MXU_DTYPE)
        for c in range(nc):
            y = jnp.dot(a, w[:, c * tc:(c + 1) * tc], preferred_element_type=F32)
            if first:
                acc_ref[cur, c] = y
            else:
                acc_ref[cur, c] += y

    do_e = jnp.logical_and(has_prev, s < nc)
    do_k = i < nt
    for e_on in (True, False):
        for first in (True, False, None):
            cond = do_e if e_on else jnp.logical_not(do_e)
            if first is None:
                cond = jnp.logical_and(cond, jnp.logical_not(do_k))
            else:
                cond = jnp.logical_and(jnp.logical_and(cond, do_k), (s == 0) if first else (s > 0))
            if not e_on and first is None:
                continue

            @pl.when(cond)
            def _(e_on=e_on, first=first):
                if e_on:
                    epilogue_chunk()
                if first is not None:
                    k_step(first)

    if want_h:
        @pl.when(jnp.logical_and(has_prev, s == nc - 1))
        def _():
            os_ref[...] = lax.rsqrt(s2_ref[...] / n_total + EPS)


def _resid(a, w, x, g_post, g_next, *, tm=1024, tk=512, tc=512):
    m, k = a.shape
    n = w.shape[1]
    tm, tk, tc = _tile(m, tm), _tile(k, tk, LANES), _tile(n, tc, LANES)
    nt, nk, nc = m // tm, k // tk, n // tc
    assert nk >= nc, (nk, nc)
    want_h = g_next is not None
    if not want_h:
        g_next = g_post
    ti = lambda i: jnp.minimum(i, nt - 1)
    kk = lambda i, s: jnp.where(i < nt, s, nk - 1)
    pi = lambda i: jnp.maximum(i - 1, 0)
    cc = lambda i, s: jnp.where(i > 0, jnp.minimum(s, nc - 1), 0)
    out_shape = [jax.ShapeDtypeStruct((m, n), F32)]
    out_specs = [pl.BlockSpec((tm, tc), lambda i, s: (pi(i), cc(i, s)))]
    if want_h:
        out_shape += [jax.ShapeDtypeStruct((m, n), MXU_DTYPE), jax.ShapeDtypeStruct((m, 1), F32)]
        out_specs += [pl.BlockSpec((tm, tc), lambda i, s: (pi(i), cc(i, s))),
                      pl.BlockSpec((tm, 1), lambda i, s: (pi(i), 0))]
    outs = pl.pallas_call(
        functools.partial(_resid_kernel, nt=nt, nk=nk, nc=nc, tc=tc, want_h=want_h),
        out_shape=out_shape,
        grid=(nt + 1, nk),
        in_specs=[pl.BlockSpec((tm, tk), lambda i, s: (ti(i), kk(i, s))),
                  pl.BlockSpec((tk, n), lambda i, s: (kk(i, s), 0)),
                  pl.BlockSpec((tm, tc), lambda i, s: (pi(i), cc(i, s))),
                  pl.BlockSpec((1, tc), lambda i, s: (0, cc(i, s))),
                  pl.BlockSpec((1, tc), lambda i, s: (0, cc(i, s)))],
        out_specs=out_specs,
        scratch_shapes=[pltpu.VMEM((2, nc, tm, tc), F32), pltpu.VMEM((tm, 1), F32), pltpu.VMEM((tm, 1), F32)],
        compiler_params=_params("arbitrary", "arbitrary"),
        name="resid",
    )(a, w, x, g_post.reshape(1, n), g_next.reshape(1, n))
    return tuple(outs) if want_h else (outs[0], None, None)


def _hgrn_gates(uq, uf, lb):
    f = lb + (1.0 - lb) * jax.nn.sigmoid(uf)
    logf = jnp.log(jnp.maximum(f, F_FLOOR))
    q = jax.nn.silu(uq) * (HGRN_HEAD_DIM ** -0.5)
    return q, 1.0 - f, logf


def _hgrn_finish(o, ug, gain, dtype):
    o = o * _rms_scale(o) * gain
    return (o * jax.nn.silu(ug)).astype(dtype)


def _split3(x):
    h1 = x.astype(MXU_DTYPE)
    r = x - h1.astype(F32)
    h2 = r.astype(MXU_DTYPE)
    h3 = (r - h2.astype(F32)).astype(MXU_DTYPE)
    return h1, h2, h3


def _diag_offsets(q, k, cum, row, block):
    yield 0, jnp.sum(q * k, axis=-1, keepdims=True)
    for j in range(1, block):
        inside = (row % block) >= j
        d = jnp.where(inside, cum - pltpu.roll(cum, j, 0), NEG_BIG)
        yield j, jnp.sum(q * pltpu.roll(k, j, 0) * jnp.exp(d), axis=-1, keepdims=True)


def _diag_blocks(q, k, v, cum, tmp_ref):
    g = HGRN_DIAG
    n = q.shape[0] // g
    for i, a in enumerate((q, k, v, cum)):
        tmp_ref[i] = a
    views = [[tmp_ref[i, pl.ds(r, n, stride=g), :] for r in range(g)] for i in range(4)]
    qs, ks, vs, cs = views
    for r in range(g):
        acc = jnp.sum(qs[r] * ks[r], axis=-1, keepdims=True) * vs[r]
        for r2 in range(r):
            p = qs[r] * ks[r2] * jnp.exp2(cs[r] - cs[r2])
            acc = acc + jnp.sum(p, axis=-1, keepdims=True) * vs[r2]
        tmp_ref[4, pl.ds(r, n, stride=g), :] = acc
    return tmp_ref[4]


def _hgrn_chunk(q, k, v, logf, st, tmp_ref):
    c = q.shape[0]
    row = lax.broadcasted_iota(jnp.int32, (c, 1), 0)
    rr = lax.broadcasted_iota(jnp.int32, (c, c), 0)
    cc = lax.broadcasted_iota(jnp.int32, (c, c), 1)
    tri = (cc <= rr).astype(MXU_DTYPE)
    cum = sum(jnp.dot(tri, h, preferred_element_type=F32) for h in _split3(logf)[::-1]) * LOG2_E

    o_diag = _diag_blocks(q, k, v, cum, tmp_ref)
    scores = jnp.zeros((c, c), F32)
    b = HGRN_DIAG
    while b < c:
        c3 = cum.reshape(c // (2 * b), 2 * b, cum.shape[1])
        e = jnp.exp2(-jnp.abs(c3 - c3[:, b - 1:b, :])).reshape(cum.shape)
        odd = ((row // b) % 2) == 1
        qt = jnp.where(odd, q * e, 0.0).astype(MXU_DTYPE)
        kt = jnp.where(odd, 0.0, k * e).astype(MXU_DTYPE)
        s = lax.dot_general(qt, kt, (((1,), (1,)), ((), ())), preferred_element_type=F32)
        scores = scores + jnp.where((rr // (2 * b)) == (cc // (2 * b)), s, 0.0)
        b *= 2

    vm = v.astype(MXU_DTYPE)
    qd = (q * jnp.exp2(cum)).astype(MXU_DTYPE)
    o = lax.dot_general(qd, st.astype(MXU_DTYPE), (((1,), (1,)), ((), ())), preferred_element_type=F32)
    o = o + jnp.dot(scores.astype(MXU_DTYPE), vm, preferred_element_type=F32) + o_diag
    last = cum[c - 1:c, :]
    kd = (k * jnp.exp2(last - cum)).astype(MXU_DTYPE)
    st_new = st * jnp.exp2(last) + jnp.dot(v.T.astype(MXU_DTYPE), kd, preferred_element_type=F32)
    return o, st_new


def _hgrn_prompt_kernel(uq_ref, uf_ref, ui_ref, ug_ref, lb_ref, gain_ref, o_ref, s_ref, st_ref, tmp_ref, *, chunk):
    n = pl.program_id(2)

    @pl.when(n == 0)
    def _():
        st_ref[...] = jnp.zeros(st_ref.shape, F32)

    dk = HGRN_HEAD_DIM
    for ci in range(uq_ref.shape[0] // chunk):
        rows = pl.ds(ci * chunk, chunk)
        for hh in range(uq_ref.shape[1] // dk):
            cols = slice(hh * dk, (hh + 1) * dk)
            q, k, logf = _hgrn_gates(uq_ref[rows, cols], uf_ref[rows, cols], lb_ref[:, cols])
            o, st_new = _hgrn_chunk(q, k, ui_ref[rows, cols], logf, st_ref[hh], tmp_ref.at[hh])
            st_ref[hh] = st_new
            o_ref[rows, cols] = _hgrn_finish(o, ug_ref[rows, cols], gain_ref[:, cols], o_ref.dtype)

    @pl.when(n == pl.num_programs(2) - 1)
    def _():
        for hh in range(st_ref.shape[0]):
            s_ref[hh] = st_ref[hh].T


def _hgrn_prompt(u, lb, gain, batch, seq, width, *, tb=512, hp=4):
    dk = HGRN_HEAD_DIM
    heads = width // dk
    chunk = min(HGRN_CHUNK, seq)
    tb, hp = _tile(seq, tb), _tile(heads, hp)
    assert tb % chunk == 0
    nt, hg = seq // tb, heads // hp
    u_spec = lambda c: pl.BlockSpec((tb, hp * dk), lambda b, h, n: (b * nt + n, c * hg + h))
    vec_spec = pl.BlockSpec((1, hp * dk), lambda b, h, n: (0, h))
    return pl.pallas_call(
        functools.partial(_hgrn_prompt_kernel, chunk=chunk),
        out_shape=[jax.ShapeDtypeStruct((batch * seq, width), MXU_DTYPE),
                   jax.ShapeDtypeStruct((batch, heads, dk, dk), F32)],
        grid=(batch, hg, nt),
        in_specs=[u_spec(0), u_spec(1), u_spec(2), u_spec(3), vec_spec, vec_spec],
        out_specs=[pl.BlockSpec((tb, hp * dk), lambda b, h, n: (b * nt + n, h)),
                   pl.BlockSpec((None, hp, dk, dk), lambda b, h, n: (b, h, 0, 0))],
        scratch_shapes=[pltpu.VMEM((hp, dk, dk), F32), pltpu.VMEM((hp, 5, chunk, dk), F32)],
        compiler_params=_params("parallel", "parallel", "arbitrary"),
        name="hgrn_prompt",
    )(u, u, u, u, lb.reshape(1, width), gain.reshape(1, width))


def _hgrn_sample_kernel(uq_ref, uf_ref, ui_ref, ug_ref, lb_ref, gain_ref, s0_ref, *rest, t):
    o_ref, s_ref = rest[-2:]
    n_prev = s_ref.shape[0] - 1
    for i in range(n_prev):
        s_ref[i] = rest[0][i]
    nb = s0_ref.shape[0]
    width = uq_ref.shape[1]
    dk = HGRN_HEAD_DIM
    heads = width // dk
    pad = dk - 2 * t
    assert pad >= 0
    lb = lb_ref[...]
    row = lax.broadcasted_iota(jnp.int32, (t, 1), 0)
    outs = [[] for _ in range(heads)]
    for sq in range(nb):
        rows = pl.ds(sq * t, t)
        q, k, logf = _hgrn_gates(uq_ref[rows, :], uf_ref[rows, :], lb)
        v = ui_ref[rows, :]
        cum = logf
        sh = 1
        while sh < t:
            cum = cum + jnp.where(row >= sh, pltpu.roll(cum, sh, 0), 0.0)
            sh *= 2
        last = cum[t - 1:t, :]
        qd = q * jnp.exp(cum)
        kd = k * jnp.exp(last - cum)
        dec = jnp.exp(last)
        for h in range(heads):
            cols = slice(h * dk, (h + 1) * dk)
            vh = v[:, cols]
            s0 = s0_ref[sq, h]
            o = jnp.dot(qd[:, cols].astype(MXU_DTYPE), s0.astype(MXU_DTYPE), preferred_element_type=F32)
            for j, rs in _diag_offsets(q[:, cols], k[:, cols], cum[:, cols], row, t):
                o = o + rs * (vh if j == 0 else pltpu.roll(vh, j, 0))
            outs[h].append(o)
            stack_t = jnp.concatenate([kd[:, cols], jnp.broadcast_to(dec[:, cols], (t, dk)),
                                       jnp.zeros((pad, dk), F32)], axis=0).T
            v_pad = jnp.concatenate([vh, jnp.zeros((dk - t, dk), F32)], axis=0)
            s_ref[n_prev, sq, h] = stack_t[:, t:t + 1] * s0 + jnp.dot(stack_t.astype(MXU_DTYPE), v_pad.astype(MXU_DTYPE),
                                                               preferred_element_type=F32)
    for h in range(heads):
        cols = slice(h * dk, (h + 1) * dk)
        o = jnp.concatenate(outs[h], axis=0)
        o_ref[:, cols] = _hgrn_finish(o, ug_ref[:, cols], gain_ref[:, cols], o_ref.dtype)


def _hgrn_sample(u, lb, gain, s0, l, prev, row0, batch, t, width, *, nb=2):
    dk = HGRN_HEAD_DIM
    heads = width // dk
    nb = _tile(batch, nb)
    assert row0 % (nb * t) == 0 and (prev is None) == (l == 0)
    r0 = row0 // (nb * t)
    u_spec = lambda c: pl.BlockSpec((nb * t, width), lambda b: (r0 + b, c))
    vec_spec = pl.BlockSpec((1, width), lambda b: (0, 0))
    stack_spec = lambda n: pl.BlockSpec((n, nb, heads, dk, dk), lambda b: (0, b, 0, 0, 0))
    in_specs = [u_spec(0), u_spec(1), u_spec(2), u_spec(3), vec_spec, vec_spec,
                pl.BlockSpec((None, nb, heads, dk, dk), lambda b: (l, b, 0, 0, 0))]
    args = [u, u, u, u, lb.reshape(1, width), gain.reshape(1, width), s0]
    if prev is not None:
        in_specs.append(stack_spec(l))
        args.append(prev)
    return pl.pallas_call(
        functools.partial(_hgrn_sample_kernel, t=t),
        out_shape=[jax.ShapeDtypeStruct((batch * t, width), MXU_DTYPE),
                   jax.ShapeDtypeStruct((l + 1, batch, heads, dk, dk), F32)],
        grid=(batch // nb,),
        in_specs=in_specs,
        out_specs=[pl.BlockSpec((nb * t, width), lambda b: (b, 0)), stack_spec(l + 1)],
        compiler_params=_params("parallel"),
        name="hgrn_sample",
    )(*args)


def _pool_kernel(p_ref, buf_ref, w_ref, scale_ref, o_ref, nbuf_ref, ext_ref, *, n_past):
    n = pl.program_id(1)
    nb, tb, width = p_ref.shape
    gw = width // len(POOL_WINDOWS)
    hist = POOL_HIST

    @pl.when(n == 0)
    def _():
        ext_ref[:, 0:hist, :] = buf_ref[...]

    @pl.when(n > 0)
    def _():
        ext_ref[:, 0:hist, :] = ext_ref[:, tb:tb + hist, :]

    ext_ref[:, hist:hist + tb, :] = p_ref[...]
    pos = n_past + n * tb + lax.broadcasted_iota(jnp.int32, (1, tb, 1), 1)
    for g, win in enumerate(POOL_WINDOWS):
        cols = slice(g * gw, (g + 1) * gw)
        tot = ext_ref[:, hist:hist + tb, cols]
        x = tot
        for i in range(1, win):
            tot = tot + ext_ref[:, hist - i:hist - i + tb, cols]
        count = jnp.minimum(pos + 1, win).astype(F32)
        pooled = (tot / count - x).reshape(nb * tb, gw)
        mixed = jnp.dot(pooled.astype(MXU_DTYPE), w_ref[g].astype(MXU_DTYPE), preferred_element_type=F32)
        o_ref[:, cols] = (mixed * scale_ref[:, cols]).astype(o_ref.dtype)

    @pl.when(n == pl.num_programs(1) - 1)
    def _():
        nbuf_ref[...] = ext_ref[:, tb:tb + hist, :]


def _pool(u, buf, w_grp, l, scale, col, row0, batch, seq, n_past, *, nb, tb):
    _, groups, gw, _ = w_grp.shape
    width = groups * gw
    hist = POOL_HIST
    nb, tb = _tile(batch, nb), _tile(seq, tb)
    nt = seq // tb
    assert col % width == 0 and row0 % (nb * tb) == 0 and (nb == 1 or nt == 1) and tb % 8 == 0
    if buf is None:
        buf = jnp.zeros((batch, hist, width), F32)
    else:
        buf = jnp.pad(buf, ((0, 0), (1, 0), (0, 0)))
    u3 = u.reshape(u.shape[0] // tb, tb, u.shape[1])
    r0 = row0 // (nb * tb)
    mixed, nbuf = pl.pallas_call(
        functools.partial(_pool_kernel, n_past=n_past),
        out_shape=[jax.ShapeDtypeStruct((batch * seq, width), MXU_DTYPE),
                   jax.ShapeDtypeStruct((batch, hist, width), F32)],
        grid=(batch // nb, nt),
        in_specs=[pl.BlockSpec((nb, tb, width), lambda b, n: (r0 + b * nt + n, 0, col // width)),
                  pl.BlockSpec((nb, hist, width), lambda b, n: (b, 0, 0)),
                  pl.BlockSpec((None, groups, gw, gw), lambda b, n: (l, 0, 0, 0)),
                  pl.BlockSpec((1, width), lambda b, n: (0, 0))],
        out_specs=[pl.BlockSpec((nb * tb, width), lambda b, n: (b * nt + n, 0)),
                   pl.BlockSpec((nb, hist, width), lambda b, n: (b, 0, 0))],
        scratch_shapes=[pltpu.VMEM((nb, hist + tb, width), F32)],
        compiler_params=_params("parallel", "arbitrary"),
        name="pool",
    )(u3, buf, w_grp, scale.reshape(1, width))
    return mixed, nbuf[:, 1:, :]


def _attend_kernel(q_ref, k_ref, v_ref, o_ref, *, tq, hd):
    nb = k_ref.shape[0]
    for b in range(nb):
        rows = slice(b * tq, (b + 1) * tq)
        for h in range(q_ref.shape[1] // hd):
            cols = slice(h * hd, (h + 1) * hd)
            k, v = k_ref[b, :, cols], v_ref[b, :, cols]
            q = q_ref[rows, cols].astype(MXU_DTYPE)
            s = lax.dot_general(q, k.astype(MXU_DTYPE), (((1,), (1,)), ((), ())),
                                preferred_element_type=F32) * (hd ** -0.5)
            e = jnp.exp(s - jnp.max(s, axis=-1, keepdims=True))
            a = e / jnp.sum(e, axis=-1, keepdims=True)
            o_ref[rows, cols] = jnp.dot(a.astype(MXU_DTYPE), v.astype(MXU_DTYPE),
                                        preferred_element_type=F32).astype(o_ref.dtype)


def _attend(u, k_arr, v_arr, k_spec, v_spec, hd, hps, col, row0, batch, seq, *, nb, tq):
    nb, tq = _tile(batch, nb), _tile(seq, tq)
    nt = seq // tq
    wq = hps * hd
    assert col % wq == 0 and row0 % (nb * tq) == 0 and (nb == 1 or nt == 1) and MEM_HEADS % hps == 0
    r0, c0 = row0 // (nb * tq), col // wq
    return pl.pallas_call(
        functools.partial(_attend_kernel, tq=tq, hd=hd),
        out_shape=jax.ShapeDtypeStruct((batch * seq, MEM_HEADS * hd), MXU_DTYPE),
        grid=(batch // nb, nt, MEM_HEADS // hps),
        in_specs=[pl.BlockSpec((nb * tq, wq), lambda b, n, h: (r0 + b * nt + n, c0 + h)), k_spec(nb), v_spec(nb)],
        out_specs=pl.BlockSpec((nb * tq, wq), lambda b, n, h: (b * nt + n, h)),
        compiler_params=_params("parallel", "arbitrary", "arbitrary"),
        name="attend",
    )(u, k_arr, v_arr)


def _cache_rows(cache):
    depth, batch, tokens, heads, hd = cache.shape
    lt = hd // LANES
    c = cache.reshape(depth, batch, tokens, heads, lt, LANES).transpose(0, 1, 2, 4, 3, 5)
    return c.reshape(depth, batch, tokens * lt * heads, LANES)


def _attend_rows_kernel(q_ref, k_ref, v_ref, o_ref, *, tq, hd):
    nb, n_rows, _ = k_ref.shape
    nlt = hd // LANES
    rper = MEM_HEADS * nlt
    nq = MEM_HEADS * tq
    row = lax.broadcasted_iota(jnp.int32, (nq, n_rows), 0)
    col = lax.broadcasted_iota(jnp.int32, (nq, n_rows), 1)
    valid = (col % rper) == (row // tq)
    for b in range(nb):
        rows = slice(b * tq, (b + 1) * tq)
        qst = jnp.concatenate([q_ref[rows, h * hd + lt * LANES:h * hd + (lt + 1) * LANES]
                               for lt in range(nlt) for h in range(MEM_HEADS)], axis=0).astype(MXU_DTYPE)
        p = lax.dot_general(qst, k_ref[b].astype(MXU_DTYPE), (((1,), (1,)), ((), ())), preferred_element_type=F32)
        s = p[0:nq]
        for lt in range(1, nlt):
            s = s + pltpu.roll(p[lt * nq:(lt + 1) * nq], n_rows - lt * MEM_HEADS, 1)
        s = jnp.where(valid, s * (hd ** -0.5), NEG_BIG)
        e = jnp.exp(s - jnp.max(s, axis=-1, keepdims=True))
        a = e / jnp.sum(e, axis=-1, keepdims=True)
        ast = jnp.concatenate([a if lt == 0 else pltpu.roll(a, lt * MEM_HEADS, 1) for lt in range(nlt)], axis=0)
        o = jnp.dot(ast.astype(MXU_DTYPE), v_ref[b].astype(MXU_DTYPE), preferred_element_type=F32)
        for lt in range(nlt):
            for h in range(MEM_HEADS):
                r0 = lt * nq + h * tq
                o_ref[rows, h * hd + lt * LANES:h * hd + (lt + 1) * LANES] = o[r0:r0 + tq].astype(o_ref.dtype)


def _attend_rows(u, k_rows, v_rows, l, hd, col, row0, batch, tq, *, nb):
    width = MEM_HEADS * hd
    nb = _tile(batch, nb)
    n_rows = k_rows.shape[2]
    assert col % width == 0 and row0 % (nb * tq) == 0 and hd % LANES == 0
    r0, c0 = row0 // (nb * tq), col // width
    kv_spec = pl.BlockSpec((None, nb, n_rows, LANES), lambda b: (l, b, 0, 0))
    return pl.pallas_call(
        functools.partial(_attend_rows_kernel, tq=tq, hd=hd),
        out_shape=jax.ShapeDtypeStruct((batch * tq, width), MXU_DTYPE),
        grid=(batch // nb,),
        in_specs=[pl.BlockSpec((nb * tq, width), lambda b: (r0 + b, c0)), kv_spec, kv_spec],
        out_specs=pl.BlockSpec((nb * tq, width), lambda b: (b, 0)),
        compiler_params=_params("parallel"),
        name="attend_rows",
    )(u, k_rows, v_rows)


def kernel(x_prompt, x_sample, state_hgrn, state_pool, cache_mem_k, cache_mem_v, mem_prompt, norm_pre_mix, norm_post_mix, norm_pre_mlp, norm_post_mlp, norm_mem, w_in, hgrn_lb, hgrn_out_norm, w_pool, pool_scale, w_mem_kv, w_branch_hgrn, w_branch_pool, w_branch_mem, w_out, w_up, w_down):
    bp, sp, d = x_prompt.shape
    bs, ss, _ = x_sample.shape
    depth = w_in.shape[0]
    hw, pw, mw = hgrn_lb.shape[1], pool_scale.shape[1], w_branch_mem.shape[1]
    mp, ms = bp * sp, bs * ss
    tokens = mem_prompt.shape[1]
    heads = hw // HGRN_HEAD_DIM
    pool_col, mem_col, gate_col = 4 * hw, 4 * hw + pw, 4 * hw + pw + mw

    lb_soft = jax.nn.softmax(hgrn_lb.astype(F32), axis=0)
    lb_all = jnp.cumsum(lb_soft, axis=0) - lb_soft[0:1]

    x, h = _rmsnorm_stack(x_prompt.reshape(mp, d), x_sample.reshape(ms, d), norm_pre_mix[0])
    mem2 = mem_prompt.reshape(bp * tokens, d)
    h_scale = None
    hd = mw // MEM_HEADS
    if hd % LANES == 0:
        k_rows, v_rows = _cache_rows(cache_mem_k), _cache_rows(cache_mem_v)
    outs = [[] for _ in range(5)]
    st_s = None
    for l in range(depth):
        kv = _proj(_rmsnorm(mem2, norm_mem[l]), w_mem_kv, l, out_dtype=F32, tm=1024)
        kv3 = kv.reshape(bp, tokens, 2 * mw)
        u, w_out_l = _proj(h, w_in, l, h_scale, out_dtype=F32, cast_src=w_out)

        o_p, st_p = _hgrn_prompt(u, lb_all[l], hgrn_out_norm[l], bp, sp, hw)
        o_s, st_s = _hgrn_sample(u, lb_all[l], hgrn_out_norm[l], state_hgrn, l, st_s, mp, bs, ss, hw)
        pool_p, buf_p = _pool(u, None, w_pool, l, pool_scale[l], pool_col, 0, bp, sp, 0, nb=1, tb=512)
        pool_s, buf_s = _pool(u, state_pool[l], w_pool, l, pool_scale[l], pool_col, mp, bs, ss, PAST_LEN,
                              nb=32, tb=ss)
        kv_spec = lambda c: lambda nb: pl.BlockSpec((nb, tokens, mw), lambda b, n, h: (b, 0, c))
        mem_p = _attend(u, kv3, kv3, kv_spec(0), kv_spec(1), hd, MEM_HEADS, mem_col, 0, bp, sp, nb=1, tq=512)
        if hd % LANES == 0:
            mem_s = _attend_rows(u, k_rows, v_rows, l, hd, mem_col, mp, bs, ss, nb=4)
        else:
            mem_s = _attend(u, cache_mem_k[l].reshape(bs, tokens, mw), cache_mem_v[l].reshape(bs, tokens, mw),
                            kv_spec(0), kv_spec(0), hd, MEM_HEADS, mem_col, mp, bs, ss, nb=4, tq=ss)

        merged = _merge((o_p, pool_p, mem_p), (o_s, pool_s, mem_s), w_branch_hgrn, w_branch_pool, w_branch_mem, l,
                        u, gate_col)
        x, h2, h2_scale = _resid(merged, w_out_l, x, norm_post_mix[l], norm_pre_mlp[l])
        ff, w_down_l = _proj(h2, w_up, l, h2_scale, out_dtype=MXU_DTYPE, relu2=True, cast_src=w_down)
        x, h, h_scale = _resid(ff, w_down_l, x, norm_post_mlp[l], norm_pre_mix[l + 1] if l + 1 < depth else None)

        for lst, val in zip(outs, (st_p, buf_p, kv3[..., :mw].reshape(bp, tokens, MEM_HEADS, hd),
                                   kv3[..., mw:].reshape(bp, tokens, MEM_HEADS, hd), buf_s)):
            lst.append(val)

    stacked = [jnp.stack(o) for o in outs]
    return (x[:mp].reshape(bp, sp, d), x[mp:].reshape(bs, ss, d), *stacked[:4], st_s, stacked[4])
```
